```python
import jax, jax.numpy as jnp
from jax import lax
import numpy as np

D_MODEL = 1024
BATCH = 8
SEQ = 4096
DEPTH = 2

GRID_W = 64
CTX_LEN = 256
EPS = 1e-6
N_MOD = 9
D_FF = 2816
BRANCH_W = 512
N_BRANCH = 3
MLA_HEADS = 8
MLA_Q_RANK = 256
MLA_KV_RANK = 128
MLA_NOPE = 64
MLA_ROPE = 32
MLA_V = 64
MLA_QK = MLA_NOPE + MLA_ROPE
AXIS_DIM = MLA_ROPE // 2
ROPE_THETA = 10000.0
Q_BLOCK = 128
POOL_WINDOWS = (2, 4, 8, 16)
POOL_GROUPS = len(POOL_WINDOWS)
POOL_GDIM = BRANCH_W // POOL_GROUPS
GLA_HEADS = 4
GLA_DK = 64
GLA_DV = 128
GLA_GATE_RANK = 16
GLA_TAU = 16.0
GLA_CHUNK = 64
IN_SPLITS = (MLA_Q_RANK, MLA_KV_RANK, MLA_ROPE,
             BRANCH_W,
             GLA_HEADS * GLA_DK, GLA_HEADS * GLA_DK,
             GLA_HEADS * GLA_DV,
             2 * GLA_GATE_RANK,
             GLA_HEADS * GLA_DV,
             N_BRANCH * D_MODEL)
D_IN = sum(IN_SPLITS)

kernel_name = "hybrid_mla_pool_gla_macaron_dit"

F32 = jnp.float32


def rmsnorm(x, g):
    xf = x.astype(F32)
    y = xf * lax.rsqrt(jnp.mean(xf * xf, axis=-1, keepdims=True) + EPS)
    return (y * g.astype(F32)).astype(x.dtype)


def modulate(x, shift, scale):
    return x * (1 + scale) + shift


def ffn_half_step(h, g, shift, scale, gate, w1, w3, w2):
    u = modulate(rmsnorm(h, g), shift, scale)
    return h + 0.5 * gate * ((jax.nn.silu(u @ w1) * (u @ w3)) @ w2)


def split_cols(z):
    idx = [int(i) for i in np.cumsum(IN_SPLITS)[:-1]]
    return jnp.split(z, idx, axis=-1)


def rope_tables(L):
    rows = L // GRID_W
    r, col = jnp.meshgrid(jnp.arange(rows, dtype=F32), jnp.arange(GRID_W, dtype=F32), indexing="ij")
    inv = ROPE_THETA ** (-jnp.arange(0, AXIS_DIM, 2, dtype=F32) / AXIS_DIM)
    pos = jnp.stack([r.reshape(-1), col.reshape(-1)], axis=-1)
    ang = pos[:, :, None] * inv
    return jnp.cos(ang), jnp.sin(ang)


def apply_rope2d(x, cos, sin):
    shp = x.shape
    xf = x.astype(F32).reshape(shp[:-1] + (2, 2, AXIS_DIM // 2))
    bshape = (shp[1],) + (1,) * (x.ndim - 3) + (2, AXIS_DIM // 2)
    cs, sn = cos.reshape(bshape), sin.reshape(bshape)
    x1, x2 = xf[..., 0, :], xf[..., 1, :]
    out = jnp.stack([x1 * cs - x2 * sn, x2 * cs + x1 * sn], axis=-2)
    return out.reshape(shp).astype(x.dtype)


def mla_project(cq, ckv, kr, g_cq, w_uq, g_ckv, w_ukv, g_qn, g_kn, rope):
    B, L = cq.shape[:2]
    q = (rmsnorm(cq, g_cq) @ w_uq).reshape(B, L, MLA_HEADS, MLA_QK)
    kv = (rmsnorm(ckv, g_ckv) @ w_ukv).reshape(B, L, MLA_HEADS, MLA_NOPE + MLA_V)
    k = jnp.concatenate([kv[..., :MLA_NOPE],
                         jnp.broadcast_to(kr[:, :, None, :], (B, L, MLA_HEADS, MLA_ROPE))], axis=-1)
    q, k = rmsnorm(q, g_qn), rmsnorm(k, g_kn)
    if rope is not None:
        cos, sin = rope
        q = jnp.concatenate([q[..., :MLA_NOPE], apply_rope2d(q[..., MLA_NOPE:], cos, sin)], axis=-1)
        k = jnp.concatenate([k[..., :MLA_NOPE], apply_rope2d(k[..., MLA_NOPE:], cos, sin)], axis=-1)
    return q, k, kv[..., MLA_NOPE:]


def attend(q, k, v):
    s = jnp.einsum("bqhd,bkhd->bhqk", q.astype(F32), k.astype(F32)) * (MLA_QK ** -0.5)
    p = jax.nn.softmax(s, axis=-1)
    return jnp.einsum("bhqk,bkhd->bqhd", p, v.astype(F32)).astype(v.dtype)


def blocked_attend(q, k, v):
    B, L, H, Dq = q.shape
    nb = L // Q_BLOCK
    qb = q.reshape(B, nb, Q_BLOCK, H, Dq).transpose(1, 0, 2, 3, 4)
    out = lax.map(lambda qi: attend(qi, k, v), qb)
    return out.transpose(1, 0, 2, 3, 4).reshape(B, L, H, v.shape[-1])


def multiscale_pool(p, w_pool, pool_scale):
    B, L, _ = p.shape
    pf = p.astype(F32)
    S = jnp.concatenate([jnp.zeros((B, 1, BRANCH_W), F32), jnp.cumsum(pf, axis=1)], axis=1)
    t = jnp.arange(L)
    outs = []
    for g, w in enumerate(POOL_WINDOWS):
        lo = jnp.clip(t - w // 2, 0, L)
        hi = jnp.clip(t + w // 2, 0, L)
        sl = slice(g * POOL_GDIM, (g + 1) * POOL_GDIM)
        Sg = S[..., sl]
        mean = (Sg[:, hi] - Sg[:, lo]) / (hi - lo).astype(F32)[None, :, None]
        outs.append(mean - pf[..., sl])
    pooled = jnp.stack(outs, axis=2).astype(p.dtype)
    y = jnp.einsum("blgc,gcd->blgd", pooled, w_pool).reshape(B, L, BRANCH_W)
    return y * pool_scale


def gla_prepare(q, k, v, a_lr, w_a2, b_a2):
    B, L = q.shape[:2]
    qh = q.astype(F32).reshape(B, L, GLA_HEADS, GLA_DK) * (GLA_DK ** -0.5)
    kh = k.astype(F32).reshape(B, L, GLA_HEADS, GLA_DK)
    vh = v.astype(F32).reshape(B, L, GLA_HEADS, GLA_DV)
    logit = jnp.einsum("bldr,drk->bldk", a_lr.reshape(B, L, 2, GLA_GATE_RANK), w_a2) + b_a2
    log_a = (jax.nn.log_sigmoid(logit.astype(F32)) / GLA_TAU).reshape(B, L, 2, GLA_HEADS, GLA_DK)
    return qh, kh, vh, log_a


def gla_chunk_scan(q, k, v, log_a, s0):
    B, L, H, DK = q.shape
    DV = v.shape[-1]
    N, C = L // GLA_CHUNK, GLA_CHUNK

    def chunks(t):
        return t.reshape(B, N, C, H, t.shape[-1]).transpose(0, 3, 1, 2, 4)

    qc, kc, vc, ac = chunks(q), chunks(k), chunks(v), chunks(log_a)
    b = jnp.cumsum(ac, axis=3)
    b_last = b[:, :, :, -1:, :]
    q_t = qc * jnp.exp(b)
    k_t = kc * jnp.exp(-b)
    k_end = kc * jnp.exp(b_last - b)
    mask = jnp.tril(jnp.ones((C, C), dtype=bool))
    A = jnp.where(mask, jnp.einsum("bhnik,bhnjk->bhnij", q_t, k_t), 0.0)
    o_intra = jnp.einsum("bhnij,bhnjv->bhniv", A, vc)
    dS = jnp.einsum("bhnjk,bhnjv->nbhkv", k_end, vc)
    decay = jnp.exp(b_last[:, :, :, 0, :]).transpose(2, 0, 1, 3)

    def step(S, inp):
        d, ds = inp
        return d[..., None] * S + ds, S

    s_final, s_starts = lax.scan(step, s0, (decay, dS))
    o_inter = jnp.einsum("bhnik,nbhkv->bhniv", q_t, s_starts)
    o = (o_intra + o_inter).transpose(0, 2, 3, 1, 4).reshape(B, L, H, DV)
    return o, s_final


def gla_bidir(q, k, v, log_a, s0_f, s0_b):
    o_f, s_f = gla_chunk_scan(q, k, v, log_a[:, :, 0], s0_f)
    fl = lambda t: jnp.flip(t, axis=1)
    o_b, s_b = gla_chunk_scan(fl(q), fl(k), fl(v), fl(log_a[:, :, 1]), s0_b)
    return o_f + fl(o_b), s_f, s_b


def gla_output(o, r, g):
    B, L = o.shape[:2]
    on = rmsnorm(o, g).reshape(B, L, GLA_HEADS * GLA_DV).astype(r.dtype)
    return on * jax.nn.silu(r)


def merge_branches(gates, o_a, o_p, o_g, w_branch, w_out):
    B, L = gates.shape[:2]
    g = jax.nn.sigmoid(gates.astype(F32)).astype(gates.dtype).reshape(B, L, N_BRANCH, D_MODEL)
    m = (g[:, :, 0] * (o_a @ w_branch[0])
         + g[:, :, 1] * (o_p @ w_branch[1])
         + g[:, :, 2] * (o_g @ w_branch[2]))
    return m @ w_out


def token_mixer(u, uc, rope, w_in, g_cq, w_uq, g_ckv, w_ukv, g_qn, g_kn, w_pool, pool_scale,
                w_a2, b_a2, g_gla_o, w_branch, w_out, need_ctx):
    B, L, _ = u.shape
    cq, ckv, kr, pz, gq, gk, gv, ga, gr, gates = split_cols(u @ w_in)
    cq_c, ckv_c, kr_c, pz_c, gq_c, gk_c, gv_c, ga_c, gr_c, gates_c = split_cols(uc @ w_in)
    q, k, v = mla_project(cq, ckv, kr, g_cq, w_uq, g_ckv, w_ukv, g_qn, g_kn, rope)
    q_c, k_c, v_c = mla_project(cq_c, ckv_c, kr_c, g_cq, w_uq, g_ckv, w_ukv, g_qn, g_kn, None)
    k_all = jnp.concatenate([k_c, k], axis=1)
    v_all = jnp.concatenate([v_c, v], axis=1)
    o_a = blocked_attend(q, k_all, v_all).reshape(B, L, MLA_HEADS * MLA_V)
    o_p = multiscale_pool(pz, w_pool, pool_scale)
    qg, kg, vg, la = gla_prepare(gq, gk, gv, ga, w_a2, b_a2)
    qg_c, kg_c, vg_c, la_c = gla_prepare(gq_c, gk_c, gv_c, ga_c, w_a2, b_a2)
    s0 = jnp.zeros((uc.shape[0], GLA_HEADS, GLA_DK, GLA_DV), F32)
    og_c, s_f, s_b = gla_bidir(qg_c, kg_c, vg_c, la_c, s0, s0)
    og, _, _ = gla_bidir(qg, kg, vg, la, s_f, s_b)
    o_g = gla_output(og, gr, g_gla_o)
    y = merge_branches(gates, o_a, o_p, o_g, w_branch, w_out)
    if not need_ctx:
        return y, None
    Lc = uc.shape[1]
    o_a_c = attend(q_c, k_c, v_c).reshape(uc.shape[0], Lc, MLA_HEADS * MLA_V)
    o_p_c = multiscale_pool(pz_c, w_pool, pool_scale)
    o_g_c = gla_output(og_c, gr_c, g_gla_o)
    yc = merge_branches(gates_c, o_a_c, o_p_c, o_g_c, w_branch, w_out)
    return y, yc


def setup_inputs(seed: int = 0) -> dict:
    key = jax.random.key(seed)
    it = iter(jax.random.split(key, 40))

    def nrm(shape, scale):
        return jax.random.normal(next(it), shape, F32) * scale

    def gain(shape):
        return 1.0 + 0.1 * jax.random.normal(next(it), shape, F32)

    D = D_MODEL
    return {
        "x": nrm((BATCH, SEQ, D), 1.0),
        "c": nrm((BATCH, D), 1.0),
        "ctx": nrm((BATCH, CTX_LEN, D), 1.0),
        "c_ctx": nrm((D,), 1.0),
        "w_ada": nrm((DEPTH, D, N_MOD * D), 0.5 * D ** -0.5),
        "b_ada": nrm((DEPTH, N_MOD * D), 0.02),
        "g_ffn1": gain((DEPTH, D)),
        "ffn1_w1": nrm((DEPTH, D, D_FF), D ** -0.5),
        "ffn1_w3": nrm((DEPTH, D, D_FF), D ** -0.5),
        "ffn1_w2": nrm((DEPTH, D_FF, D), D_FF ** -0.5),
        "g_mix": gain((DEPTH, D)),
        "w_in": nrm((DEPTH, D, D_IN), D ** -0.5),
        "g_cq": gain((DEPTH, MLA_Q_RANK)),
        "w_uq": nrm((DEPTH, MLA_Q_RANK, MLA_HEADS * MLA_QK), MLA_Q_RANK ** -0.5),
        "g_ckv": gain((DEPTH, MLA_KV_RANK)),
        "w_ukv": nrm((DEPTH, MLA_KV_RANK, MLA_HEADS * (MLA_NOPE + MLA_V)), MLA_KV_RANK ** -0.5),
        "g_qn": gain((DEPTH, MLA_QK)),
        "g_kn": gain((DEPTH, MLA_QK)),
        "w_pool": nrm((DEPTH, POOL_GROUPS, POOL_GDIM, POOL_GDIM), POOL_GDIM ** -0.5),
        "pool_scale": gain((DEPTH, BRANCH_W)),
        "w_a2": nrm((DEPTH, 2, GLA_GATE_RANK, GLA_HEADS * GLA_DK), GLA_GATE_RANK ** -0.5),
        "b_a2": nrm((DEPTH, 2, GLA_HEADS * GLA_DK), 0.5),
        "g_gla_o": gain((DEPTH, GLA_DV)),
        "w_branch": nrm((DEPTH, N_BRANCH, BRANCH_W, D), BRANCH_W ** -0.5),
        "w_out": nrm((DEPTH, D, D), D ** -0.5),
        "g_ffn2": gain((DEPTH, D)),
        "ffn2_w1": nrm((DEPTH, D, D_FF), D ** -0.5),
        "ffn2_w3": nrm((DEPTH, D, D_FF), D ** -0.5),
        "ffn2_w2": nrm((DEPTH, D_FF, D), D_FF ** -0.5),
    }


def reference(x, c, ctx, c_ctx, w_ada, b_ada, g_ffn1, ffn1_w1, ffn1_w3, ffn1_w2, g_mix, w_in,
              g_cq, w_uq, g_ckv, w_ukv, g_qn, g_kn, w_pool, pool_scale, w_a2, b_a2, g_gla_o,
              w_branch, w_out, g_ffn2, ffn2_w1, ffn2_w3, ffn2_w2):
    B, L, D = x.shape
    rope = rope_tables(L)
    h, hc = x, ctx
    sc, scc = jax.nn.silu(c), jax.nn.silu(c_ctx)
    for l in range(DEPTH):
        last = l == DEPTH - 1
        mod = (sc @ w_ada[l] + b_ada[l]).reshape(B, 1, N_MOD, D)
        mod_c = (scc @ w_ada[l] + b_ada[l]).reshape(1, 1, N_MOD, D)
        h = ffn_half_step(h, g_ffn1[l], mod[:, :, 0], mod[:, :, 1], mod[:, :, 2],
                          ffn1_w1[l], ffn1_w3[l], ffn1_w2[l])
        hc = ffn_half_step(hc, g_ffn1[l], mod_c[:, :, 0], mod_c[:, :, 1], mod_c[:, :, 2],
                           ffn1_w1[l], ffn1_w3[l], ffn1_w2[l])
        u = modulate(rmsnorm(h, g_mix[l]), mod[:, :, 3], mod[:, :, 4])
        uc = modulate(rmsnorm(hc, g_mix[l]), mod_c[:, :, 3], mod_c[:, :, 4])
        y, yc = token_mixer(u, uc, rope, w_in[l], g_cq[l], w_uq[l], g_ckv[l], w_ukv[l], g_qn[l],
                            g_kn[l], w_pool[l], pool_scale[l], w_a2[l], b_a2[l], g_gla_o[l],
                            w_branch[l], w_out[l], not last)
        h = h + mod[:, :, 5] * y
        h = ffn_half_step(h, g_ffn2[l], mod[:, :, 6], mod[:, :, 7], mod[:, :, 8],
                          ffn2_w1[l], ffn2_w3[l], ffn2_w2[l])
        if not last:
            hc = hc + mod_c[:, :, 5] * yc
            hc = ffn_half_step(hc, g_ffn2[l], mod_c[:, :, 6], mod_c[:, :, 7], mod_c[:, :, 8],
                               ffn2_w1[l], ffn2_w3[l], ffn2_w2[l])
    return h
```

```python
import functools
import math

import jax
import jax.numpy as jnp
import numpy as np
from jax import lax
from jax.experimental import pallas as pl
from jax.experimental.pallas import tpu as pltpu

F32 = jnp.float32
BF16 = jnp.bfloat16

EPS = 1e-6
GRID_W = 64
N_MOD = 9
MLA_HEADS = 8
MLA_Q_RANK = 256
MLA_KV_RANK = 128
MLA_NOPE = 64
MLA_ROPE = 32
MLA_V = 64
MLA_QK = MLA_NOPE + MLA_ROPE
AXIS_DIM = MLA_ROPE // 2
ROPE_THETA = 10000.0
POOL_WINDOWS = (2, 4, 8, 16)
BRANCH_W = 512
POOL_GDIM = BRANCH_W // len(POOL_WINDOWS)
GLA_HEADS = 4
GLA_DK = 64
GLA_DV = 128
GLA_GATE_RANK = 16
GLA_TAU = 16.0
GLA_CHUNK = 64
N_BRANCH = 3
HEAD_LANES = 128

C_CQ, C_CKV, C_KR, C_PZ, C_GQ, C_GK, C_GV, C_GA, C_GR, C_END = (
    0, 256, 384, 512, 1024, 1280, 1536, 2048, 2176, 2688)

VMEM_LIMIT = 56 * 1024 * 1024


def _cparams(sem):
    return pltpu.CompilerParams(dimension_semantics=sem, vmem_limit_bytes=VMEM_LIMIT)


def _const_spec(shape):
    nd = len(shape)
    return pl.BlockSpec(shape, lambda *_: (0,) * nd, pipeline_mode=pl.Buffered(1))


def _dot(a, b):
    return jnp.dot(a, b, preferred_element_type=F32)


def _dot_nt(a, b):
    return lax.dot_general(a, b, (((1,), (1,)), ((), ())), preferred_element_type=F32)


def _dot_tn(a, b):
    return lax.dot_general(a, b, (((0,), (0,)), ((), ())), preferred_element_type=F32)


def _silu(x):
    return x * (1.0 / (1.0 + jnp.exp(-x)))


def _sigmoid(x):
    return 1.0 / (1.0 + jnp.exp(-x))


def _rms(x, g):
    ms = jnp.mean(x * x, axis=-1, keepdims=True)
    return x * lax.rsqrt(ms + EPS) * g


def _ada_kernel(c_ref, w_ref, b_ref, o_ref):
    s = _silu(c_ref[...]).astype(BF16)
    o_ref[...] = _dot(s, w_ref[...].astype(BF16)) + b_ref[...]


def _ada_mod(cond, w_ada, b_ada):
    depth, d, n = w_ada.shape
    rows = cond.shape[0]
    tn = 1024
    return pl.pallas_call(
        _ada_kernel,
        grid=(depth, n // tn),
        in_specs=[
            pl.BlockSpec((rows, d), lambda l, j: (0, 0)),
            pl.BlockSpec((None, d, tn), lambda l, j: (l, 0, j)),
            pl.BlockSpec((None, 1, tn), lambda l, j: (l, 0, j)),
        ],
        out_specs=pl.BlockSpec((None, rows, tn), lambda l, j: (l, 0, j)),
        out_shape=jax.ShapeDtypeStruct((depth, rows, n), F32),
        compiler_params=_cparams(("arbitrary", "arbitrary")),
        name="ada_mod",
    )(cond, w_ada, b_ada.reshape(depth, 1, n))


def _mod_rows(modb_ref, modc_ref, tile_in_batch, tm, d, lc):
    modb = modb_ref[...]
    modc = modc_ref[...]
    if lc == 0:
        return modb[0:1], modb[1:2], modb[2:3]
    row = tile_in_batch * tm + lax.broadcasted_iota(jnp.int32, (tm, d), 0)
    is_ctx = row < lc
    return tuple(jnp.where(is_ctx, modc[i:i + 1], modb[i:i + 1]) for i in range(3))


def _mod_specs(mod, layer, group, tiles_per_batch, n_batch):
    d = mod.shape[-1]
    blk = (None, None, None, 3, d)
    return [
        pl.BlockSpec(blk, lambda i: (layer, i // tiles_per_batch, group, 0, 0)),
        pl.BlockSpec(blk, lambda i: (layer, n_batch, group, 0, 0)),
    ]


def _ffn_kernel(h_ref, modb_ref, modc_ref, g_ref, w1_ref, w3_ref, w2_ref, o_ref, hid_ref,
                *, tm, lc, tiles_per_batch, col_chunk):
    d = h_ref.shape[-1]
    j = pl.program_id(0) % tiles_per_batch
    shift, scale, gate = _mod_rows(modb_ref, modc_ref, j, tm, d, lc)
    h = h_ref[...]
    u = (_rms(h, g_ref[...]) * (1.0 + scale) + shift).astype(BF16)
    d_ff = w1_ref.shape[-1]
    for c in range(d_ff // col_chunk):
        cols = slice(c * col_chunk, (c + 1) * col_chunk)
        a = _dot(u, w1_ref[:, cols])
        b = _dot(u, w3_ref[:, cols])
        hid_ref[:, cols] = (_silu(a) * b).astype(BF16)
    y = _dot(hid_ref[...], w2_ref[...])
    o_ref[...] = h + 0.5 * gate * y


def _ffn(h, mod, layer, group, g, w1, w3, w2, *, n_batch, lc, l, lat_only=False):
    t_all, d = h.shape
    d_ff = w1.shape[-1]
    r = lc + l
    if lat_only:
        tm = math.gcd(lc, l) if lc else l
        tm = min(tm, 256)
        tiles_per_batch = l // tm
        skip = lc // tm
        rows_out = n_batch * l
        in_map = lambda i: ((i // tiles_per_batch) * (r // tm) + skip + i % tiles_per_batch, 0)
        lc_k = 0
    else:
        tm = r // 8 if (r % 128 == 0 and r // 8 >= 128) else r // 2
        tiles_per_batch = r // tm
        rows_out = t_all
        in_map = lambda i: (i, 0)
        lc_k = lc
    col_chunk = d_ff // 2 if (d_ff // 2) % 128 == 0 else d_ff
    kern = functools.partial(_ffn_kernel, tm=tm, lc=lc_k, tiles_per_batch=tiles_per_batch,
                             col_chunk=col_chunk)
    return pl.pallas_call(
        kern,
        grid=(rows_out // tm,),
        in_specs=[pl.BlockSpec((tm, d), in_map)]
        + _mod_specs(mod, layer, group, tiles_per_batch, n_batch)
        + [_const_spec((1, d)), _const_spec((d, d_ff)), _const_spec((d, d_ff)),
           _const_spec((d_ff, d))],
        out_specs=pl.BlockSpec((tm, d), lambda i: (i, 0)),
        out_shape=jax.ShapeDtypeStruct((rows_out, d), F32),
        scratch_shapes=[pltpu.VMEM((tm, d_ff), BF16)],
        compiler_params=_cparams(("arbitrary",)),
        name="ffn_half_step",
    )(h, mod, mod, g.reshape(1, d), w1, w3, w2)


def _head_norm_rope(x, gain, tab, lane):
    ss = jnp.sum(jnp.where(lane < MLA_QK, x * x, 0.0), axis=-1, keepdims=True) * (1.0 / MLA_QK)
    t = x * lax.rsqrt(ss + EPS) * gain * tab
    moved = pltpu.roll(t, HEAD_LANES - MLA_ROPE, axis=1)
    return jnp.where(lane < MLA_NOPE, t, jnp.where(lane < MLA_QK, t + moved, 0.0))


def _mixin_kernel(h_ref, modb_ref, modc_ref, g_ref, w_ref, gcq_ref, wuq_ref, gckv_ref,
                  wuk_ref, wuv_ref, gq_ref, gk_ref, tab_ref,
                  q_out, k_out, v_out, pz_out, gq_out, gk_out, gv_out, ga_out, gr_out,
                  *, tm, lc, tiles_per_batch):
    d = h_ref.shape[-1]
    j = pl.program_id(0) % tiles_per_batch
    shift, scale, _ = _mod_rows(modb_ref, modc_ref, j, tm, d, lc)
    u = (_rms(h_ref[...], g_ref[...]) * (1.0 + scale) + shift).astype(BF16)

    pz_out[...] = _dot(u, w_ref[:, C_PZ:C_GQ])
    gq_out[...] = _dot(u, w_ref[:, C_GQ:C_GK])
    gk_out[...] = _dot(u, w_ref[:, C_GK:C_GV])
    gv_out[...] = _dot(u, w_ref[:, C_GV:C_GA]).astype(BF16)
    ga_out[...] = _dot(u, w_ref[:, C_GA:C_GR]).astype(BF16)
    gr_out[...] = _dot(u, w_ref[:, C_GR:C_END])

    lowrank = _dot(u, w_ref[:, C_CQ:C_PZ])
    cqn = _rms(lowrank[:, C_CQ:C_CKV], gcq_ref[...]).astype(BF16)
    ckvn = _rms(lowrank[:, C_CKV:C_KR], gckv_ref[...]).astype(BF16)
    kr = lowrank[:, C_KR:C_PZ]
    q_pre = _dot(cqn, wuq_ref[...])
    k_pre = _dot(ckvn, wuk_ref[...])
    v_out[...] = _dot(ckvn, wuv_ref[...]).astype(BF16)

    tab = tab_ref[...]
    lane = lax.broadcasted_iota(jnp.int32, (tm, HEAD_LANES), 1)
    gq = gq_ref[...]
    gk = gk_ref[...]
    for hd in range(MLA_HEADS):
        cols = slice(hd * HEAD_LANES, (hd + 1) * HEAD_LANES)
        q_out[:, cols] = _head_norm_rope(q_pre[:, cols], gq, tab, lane).astype(BF16)
        k_out[:, cols] = _head_norm_rope(k_pre[:, cols] + kr, gk, tab, lane).astype(BF16)


def _mixin(h, mod, layer, pw, tab, *, n_batch, lc, l):
    t_all, d = h.shape
    r = lc + l
    tm = r // 8 if (r % 128 == 0 and r // 8 >= 128) else r // 2
    tiles_per_batch = r // tm
    kern = functools.partial(_mixin_kernel, tm=tm, lc=lc, tiles_per_batch=tiles_per_batch)
    tok = lambda w: pl.BlockSpec((tm, w), lambda i: (i, 0))
    outs = [(MLA_HEADS * HEAD_LANES, BF16), (MLA_HEADS * HEAD_LANES, BF16),
            (MLA_HEADS * MLA_V, BF16), (BRANCH_W, F32), (GLA_HEADS * GLA_DK, F32),
            (GLA_HEADS * GLA_DK, F32), (GLA_HEADS * GLA_DV, BF16), (128, BF16),
            (GLA_HEADS * GLA_DV, F32)]
    return pl.pallas_call(
        kern,
        grid=(t_all // tm,),
        in_specs=[tok(d)] + _mod_specs(mod, layer, 1, tiles_per_batch, n_batch) + [
            _const_spec((1, d)), _const_spec((d, C_END)),
            _const_spec((1, MLA_Q_RANK)), _const_spec((MLA_Q_RANK, MLA_HEADS * HEAD_LANES)),
            _const_spec((1, MLA_KV_RANK)), _const_spec((MLA_KV_RANK, MLA_HEADS * HEAD_LANES)),
            _const_spec((MLA_KV_RANK, MLA_HEADS * MLA_V)),
            _const_spec((1, HEAD_LANES)), _const_spec((1, HEAD_LANES)),
            pl.BlockSpec((tm, HEAD_LANES), lambda i: (i % tiles_per_batch, 0)),
        ],
        out_specs=[tok(w) for w, _ in outs],
        out_shape=[jax.ShapeDtypeStruct((t_all, w), dt) for w, dt in outs],
        compiler_params=_cparams(("arbitrary",)),
        name="mixer_in_proj",
    )(h, mod, mod, pw["g_mix"], pw["w_all"], pw["g_cq"], pw["w_uq"], pw["g_ckv"],
      pw["w_uk"], pw["w_uv"], pw["gq_tab"], pw["gk_tab"], tab)


def _attn_tile(q_ref, k_ref, v_ref, o_ref, n_keys):
    v2 = v_ref[0:n_keys, :]
    lane = lax.broadcasted_iota(jnp.int32, o_ref.shape, 1)
    out = None
    for hh in range(2):
        cols = slice(hh * HEAD_LANES, (hh + 1) * HEAD_LANES)
        s = _dot_nt(q_ref[:, cols], k_ref[0:n_keys, cols])
        m = jnp.max(s, axis=-1, keepdims=True)
        p = jnp.exp(s - m)
        denom = jnp.sum(p, axis=-1, keepdims=True)
        o = _dot(p.astype(BF16), v2) * (1.0 / denom)
        out = o if hh == 0 else jnp.where(lane < MLA_V, out, o)
    o_ref[...] = out.astype(o_ref.dtype)


def _attn_kernel(q_ref, k_ref, v_ref, o_ref, *, lc):
    r = k_ref.shape[0]
    i = pl.program_id(2)

    if lc:
        @pl.when(i == 0)
        def _():
            _attn_tile(q_ref, k_ref, v_ref, o_ref, lc)

    @pl.when(i > 0 if lc else i >= 0)
    def _():
        _attn_tile(q_ref, k_ref, v_ref, o_ref, r)


def _attention(q, k, v, *, n_batch, lc, l):
    t_all = q.shape[0]
    r = lc + l
    tq = lc
    tiles = r // tq
    kern = functools.partial(_attn_kernel, lc=lc)
    return pl.pallas_call(
        kern,
        grid=(n_batch, MLA_HEADS // 2, tiles),
        in_specs=[
            pl.BlockSpec((tq, 2 * HEAD_LANES), lambda b, hp, i: (b * tiles + i, hp)),
            pl.BlockSpec((r, 2 * HEAD_LANES), lambda b, hp, i: (b, hp)),
            pl.BlockSpec((r, 2 * MLA_V), lambda b, hp, i: (b, hp)),
        ],
        out_specs=pl.BlockSpec((tq, 2 * MLA_V), lambda b, hp, i: (b * tiles + i, hp)),
        out_shape=jax.ShapeDtypeStruct((t_all, MLA_HEADS * MLA_V), BF16),
        compiler_params=_cparams(("arbitrary", "arbitrary", "arbitrary")),
        name="mla_attention",
    )(q, k, v)


def _pool_kernel(p_ref, w_ref, s_ref, o_ref, *, lc, l):
    r = lc + l
    t = lax.broadcasted_iota(jnp.int32, (r, POOL_GDIM), 0)
    in_lat = t >= lc
    pos = jnp.where(in_lat, t - lc, t)
    seg_len = jnp.where(in_lat, l, lc)

    def shifted(x, dlt):
        src = pos + dlt
        rolled = pltpu.roll(x, (-dlt) % r, axis=0)
        return jnp.where(src >= 0, jnp.where(src < seg_len, rolled, 0.0), 0.0)

    for g, w in enumerate(POOL_WINDOWS):
        half = w // 2
        cols = slice(g * POOL_GDIM, (g + 1) * POOL_GDIM)
        x = p_ref[:, cols]
        right = x
        left = x
        step = 1
        while step < half:
            right = right + shifted(right, step)
            left = left + shifted(left, -step)
            step *= 2
        win = shifted(left, -1) + right
        cnt = (jnp.minimum(pos + half, seg_len) - jnp.maximum(pos - half, 0)).astype(F32)
        pooled = (win / cnt - x).astype(BF16)
        o_ref[:, cols] = (_dot(pooled, w_ref[g]) * s_ref[:, cols]).astype(o_ref.dtype)


def _pool(pz, w_pool, pool_scale, *, n_batch, lc, l):
    t_all = pz.shape[0]
    r = lc + l
    kern = functools.partial(_pool_kernel, lc=lc, l=l)
    return pl.pallas_call(
        kern,
        grid=(n_batch,),
        in_specs=[pl.BlockSpec((r, BRANCH_W), lambda b: (b, 0)),
                  _const_spec(w_pool.shape), _const_spec((1, BRANCH_W))],
        out_specs=pl.BlockSpec((r, BRANCH_W), lambda b: (b, 0)),
        out_shape=jax.ShapeDtypeStruct((t_all, BRANCH_W), BF16),
        compiler_params=_cparams(("arbitrary",)),
        name="multiscale_pool",
    )(pz, w_pool, pool_scale)


def _gla_kernel(q_ref, k_ref, v_ref, ga_ref, gr_ref, wa_ref, ba_ref, go_ref, o_ref,
                qt_s, kt_s, ke_s, tot_s, og_s, *, lc, l, blk):
    r = lc + l
    c = GLA_CHUNK
    n_chunks = r // c
    n_ctx = lc // c
    bi_ = lax.broadcasted_iota(jnp.int32, (blk, blk), 0)
    bj_ = lax.broadcasted_iota(jnp.int32, (blk, blk), 1)
    same_chunk = (bi_ // c) == (bj_ // c)
    ci_ = lax.broadcasted_iota(jnp.int32, (c, c), 0)
    cj_ = lax.broadcasted_iota(jnp.int32, (c, c), 1)
    ones_m = jnp.where(same_chunk, 1.0, 0.0).astype(BF16)

    for d in range(2):
        keep = (bj_ <= bi_) if d == 0 else (bj_ >= bi_)
        tri_m = jnp.where(same_chunk, jnp.where(keep, 1.0, 0.0), 0.0).astype(BF16)
        causal = (cj_ <= ci_) if d == 0 else (cj_ >= ci_)

        def prep(b, carry):
            rows = pl.ds(pl.multiple_of(b * blk, blk), blk)
            logit = _dot(ga_ref[rows, :], wa_ref[d]) + ba_ref[d]
            la = (jnp.minimum(logit, 0.0) - jnp.log(1.0 + jnp.exp(-jnp.abs(logit)))) * (1.0 / GLA_TAU)
            hi = la.astype(BF16)
            lo = (la - hi.astype(F32)).astype(BF16)
            hl = jnp.concatenate([hi, lo], axis=1)
            cs = _dot(tri_m, hl)
            ts = _dot(ones_m, hl)
            w = la.shape[1]
            bcum = cs[:, :w] + cs[:, w:]
            tot = ts[:, :w] + ts[:, w:]
            q = q_ref[rows, :] * (GLA_DK ** -0.5)
            k = k_ref[rows, :]
            qt_s[rows, :] = q * jnp.exp(bcum)
            kt_s[rows, :] = k * jnp.exp(-bcum)
            ke_s[rows, :] = k * jnp.exp(tot - bcum)
            tot_s[rows, :] = tot
            return carry

        lax.fori_loop(0, r // blk, prep, 0)

        for hh in range(2):
            kc = slice(hh * GLA_DK, (hh + 1) * GLA_DK)
            vc = slice(hh * GLA_DV, (hh + 1) * GLA_DV)

            def scan(i, st):
                if d == 0:
                    n = i
                else:
                    n = jnp.where(i < n_ctx, n_ctx - 1 - i, n_chunks - 1 - (i - n_ctx))
                rows = pl.ds(pl.multiple_of(n * c, c), c)
                qc = qt_s[rows, kc].astype(BF16)
                kk = kt_s[rows, kc].astype(BF16)
                ke = ke_s[rows, kc].astype(BF16)
                vv = v_ref[rows, vc]
                a = jnp.where(causal, _dot_nt(qc, kk), 0.0).astype(BF16)
                o = _dot(a, vv) + _dot_nt(qc, st.astype(BF16))
                dst = _dot_tn(vv, ke)
                dec = jnp.exp(tot_s[pl.ds(pl.multiple_of(n * c, c), 1), kc])
                if d == 0:
                    og_s[rows, vc] = o
                else:
                    og_s[rows, vc] = og_s[rows, vc] + o
                return st * dec + dst

            lax.fori_loop(0, n_chunks, scan, jnp.zeros((GLA_DV, GLA_DK), F32), unroll=2)

    for hh in range(2):
        vc = slice(hh * GLA_DV, (hh + 1) * GLA_DV)
        on = _rms(og_s[:, vc], go_ref[...])
        o_ref[:, vc] = (on * _silu(gr_ref[:, vc])).astype(o_ref.dtype)


def _gla(gq, gk, gv, ga, gr, wa, ba, g_o, *, n_batch, lc, l):
    t_all = gq.shape[0]
    r = lc + l
    blk = 256 if r % 256 == 0 else 128
    kern = functools.partial(_gla_kernel, lc=lc, l=l, blk=blk)
    seq = lambda w: pl.BlockSpec((r, w), lambda b, hp: (b, hp), pipeline_mode=pl.Buffered(1))
    return pl.pallas_call(
        kern,
        grid=(n_batch, GLA_HEADS // 2),
        in_specs=[seq(2 * GLA_DK), seq(2 * GLA_DK), seq(2 * GLA_DV),
                  pl.BlockSpec((r, 128), lambda b, hp: (b, 0), pipeline_mode=pl.Buffered(1)),
                  seq(2 * GLA_DV),
                  pl.BlockSpec((2, 128, 2 * GLA_DK), lambda b, hp: (0, 0, hp)),
                  pl.BlockSpec((2, 1, 2 * GLA_DK), lambda b, hp: (0, 0, hp)),
                  pl.BlockSpec((1, GLA_DV), lambda b, hp: (0, 0))],
        out_specs=pl.BlockSpec((r, 2 * GLA_DV), lambda b, hp: (b, hp)),
        out_shape=jax.ShapeDtypeStruct((t_all, GLA_HEADS * GLA_DV), BF16),
        scratch_shapes=[pltpu.VMEM((r, 2 * GLA_DK), F32)] * 4 + [pltpu.VMEM((r, 2 * GLA_DV), F32)],
        compiler_params=_cparams(("arbitrary", "arbitrary")),
        name="gla_bidir",
    )(gq, gk, gv, ga, gr, wa, ba, g_o)


def _merge_kernel(h_ref, modb_ref, modc_ref, g_ref, oa_ref, op_ref, og_ref, wg_ref, wb_ref,
                  wo_ref, o_ref, *, tm, lc, tiles_per_batch):
    d = h_ref.shape[-1]
    j = pl.program_id(0) % tiles_per_batch
    shift, scale, gate = _mod_rows(modb_ref, modc_ref, j, tm, d, lc)
    h = h_ref[...]
    u = (_rms(h, g_ref[...]) * (1.0 + scale) + shift).astype(BF16)
    m = None
    for br, src in enumerate((oa_ref, op_ref, og_ref)):
        gt = _sigmoid(_dot(u, wg_ref[:, br * d:(br + 1) * d]))
        term = gt * _dot(src[...], wb_ref[br])
        m = term if m is None else m + term
    y = _dot(m.astype(BF16), wo_ref[...])
    o_ref[...] = h + gate * y


def _merge(h, mod, layer, g_mix, o_a, o_p, o_g, w_gates, w_branch, w_out, *, n_batch, lc, l):
    t_all, d = h.shape
    r = lc + l
    tm = r // 8 if (r % 128 == 0 and r // 8 >= 128) else r // 2
    tiles_per_batch = r // tm
    kern = functools.partial(_merge_kernel, tm=tm, lc=lc, tiles_per_batch=tiles_per_batch)
    tok = lambda w: pl.BlockSpec((tm, w), lambda i: (i, 0))
    return pl.pallas_call(
        kern,
        grid=(t_all // tm,),
        in_specs=[tok(d)] + _mod_specs(mod, layer, 1, tiles_per_batch, n_batch) + [
            _const_spec((1, d)), tok(BRANCH_W), tok(BRANCH_W), tok(BRANCH_W),
            _const_spec(w_gates.shape), _const_spec(w_branch.shape), _const_spec(w_out.shape)],
        out_specs=tok(d),
        out_shape=jax.ShapeDtypeStruct((t_all, d), F32),
        compiler_params=_cparams(("arbitrary",)),
        name="branch_merge",
    )(h, mod, mod, g_mix, o_a, o_p, o_g, w_gates, w_branch, w_out)


_ROPE_PARTNER = np.arange(MLA_ROPE) ^ (AXIS_DIM // 2)


def _rope_table(lc, l):
    pos = jnp.arange(l)
    row = (pos // GRID_W).astype(F32)
    col = (pos % GRID_W).astype(F32)
    inv = ROPE_THETA ** (-jnp.arange(0, AXIS_DIM, 2, dtype=F32) / AXIS_DIM)
    ang = jnp.stack([row, col], axis=-1)[:, :, None] * inv
    cos = jnp.repeat(jnp.cos(ang)[:, :, None, :], 2, axis=2).reshape(l, MLA_ROPE)
    sin = jnp.repeat(jnp.sin(ang)[:, :, None, :], 2, axis=2)
    sin = (sin * jnp.array([-1.0, 1.0], F32)[None, None, :, None]).reshape(l, MLA_ROPE)
    lat = jnp.concatenate([jnp.ones((l, MLA_NOPE), F32), cos, sin], axis=1)
    ctx = jnp.concatenate([jnp.ones((lc, MLA_QK), F32), jnp.zeros((lc, MLA_ROPE), F32)], axis=1)
    return jnp.concatenate([ctx, lat], axis=0)


def _pack_layer(p, layer):
    d = p["w_in"].shape[1]
    w_in = p["w_in"][layer]
    offs = np.cumsum([0, MLA_Q_RANK, MLA_KV_RANK, MLA_ROPE, BRANCH_W, GLA_HEADS * GLA_DK,
                      GLA_HEADS * GLA_DK, GLA_HEADS * GLA_DV, 2 * GLA_GATE_RANK,
                      GLA_HEADS * GLA_DV, N_BRANCH * d])
    cq, ckv, kr, pz, gq, gk, gv, ga, gr, gates = [w_in[:, offs[i]:offs[i + 1]] for i in range(10)]
    kr_ext = jnp.concatenate([jnp.zeros((d, MLA_NOPE), F32), kr, kr[:, _ROPE_PARTNER]], axis=1)
    ga_ext = jnp.concatenate([ga, jnp.zeros((d, 128 - 2 * GLA_GATE_RANK), F32)], axis=1)
    w_all = jnp.concatenate([cq, ckv, kr_ext, pz, gq, gk, gv, ga_ext, gr], axis=1).astype(BF16)

    w_uq = p["w_uq"][layer].reshape(MLA_Q_RANK, MLA_HEADS, MLA_QK)
    w_uq = jnp.concatenate([w_uq, w_uq[:, :, MLA_NOPE + _ROPE_PARTNER]], axis=2)
    w_uq = w_uq.reshape(MLA_Q_RANK, MLA_HEADS * HEAD_LANES).astype(BF16)
    w_ukv = p["w_ukv"][layer].reshape(MLA_KV_RANK, MLA_HEADS, MLA_NOPE + MLA_V)
    w_uk = jnp.concatenate([w_ukv[:, :, :MLA_NOPE],
                            jnp.zeros((MLA_KV_RANK, MLA_HEADS, HEAD_LANES - MLA_NOPE), F32)], axis=2)
    w_uk = w_uk.reshape(MLA_KV_RANK, MLA_HEADS * HEAD_LANES).astype(BF16)
    w_uv = w_ukv[:, :, MLA_NOPE:].reshape(MLA_KV_RANK, MLA_HEADS * MLA_V).astype(BF16)

    def gain_tab(g):
        return jnp.concatenate([g, g[MLA_NOPE + _ROPE_PARTNER]])[None, :]

    wa = jnp.zeros((2, 128, GLA_HEADS * GLA_DK), F32)
    for dr in range(2):
        wa = wa.at[dr, dr * GLA_GATE_RANK:(dr + 1) * GLA_GATE_RANK].set(p["w_a2"][layer, dr])

    return dict(
        g_mix=p["g_mix"][layer][None, :], w_all=w_all,
        g_cq=p["g_cq"][layer][None, :], w_uq=w_uq,
        g_ckv=p["g_ckv"][layer][None, :], w_uk=w_uk, w_uv=w_uv,
        gq_tab=gain_tab(p["g_qn"][layer]) * (MLA_QK ** -0.5), gk_tab=gain_tab(p["g_kn"][layer]),
        w_gates=gates.astype(BF16),
        w_pool=p["w_pool"][layer].astype(BF16), pool_scale=p["pool_scale"][layer][None, :],
        wa=wa.astype(BF16), ba=p["b_a2"][layer][:, None, :], g_gla_o=p["g_gla_o"][layer][None, :],
        w_branch=p["w_branch"][layer].astype(BF16), w_out=p["w_out"][layer].astype(BF16),
    )


def kernel(x, c, ctx, c_ctx, w_ada, b_ada, g_ffn1, ffn1_w1, ffn1_w3, ffn1_w2, g_mix, w_in,
           g_cq, w_uq, g_ckv, w_ukv, g_qn, g_kn, w_pool, pool_scale, w_a2, b_a2, g_gla_o,
           w_branch, w_out, g_ffn2, ffn2_w1, ffn2_w3, ffn2_w2):
    n_batch, l, d = x.shape
    lc = ctx.shape[1]
    depth = w_ada.shape[0]
    dims = dict(n_batch=n_batch, lc=lc, l=l)
    p = dict(w_in=w_in, g_mix=g_mix, g_cq=g_cq, w_uq=w_uq, g_ckv=g_ckv, w_ukv=w_ukv, g_qn=g_qn,
             g_kn=g_kn, w_pool=w_pool, pool_scale=pool_scale, w_a2=w_a2, b_a2=b_a2,
             g_gla_o=g_gla_o, w_branch=w_branch, w_out=w_out)

    cond_rows = 16
    cond = jnp.zeros((cond_rows, d), F32).at[:n_batch].set(c).at[n_batch].set(c_ctx)
    mod = _ada_mod(cond, w_ada, b_ada).reshape(depth, cond_rows, 3, 3, d)

    tab = _rope_table(lc, l)
    h = jnp.concatenate([ctx, x], axis=1).reshape(n_batch * (lc + l), d)
    for layer in range(depth):
        last = layer == depth - 1
        pw = _pack_layer(p, layer)
        h = _ffn(h, mod, layer, 0, g_ffn1[layer], ffn1_w1[layer].astype(BF16),
                 ffn1_w3[layer].astype(BF16), ffn1_w2[layer].astype(BF16), **dims)
        q, k, v, pz, gq, gk, gv, ga, gr = _mixin(h, mod, layer, pw, tab, **dims)
        o_a = _attention(q, k, v, **dims)
        o_p = _pool(pz, pw["w_pool"], pw["pool_scale"], **dims)
        o_g = _gla(gq, gk, gv, ga, gr, pw["wa"], pw["ba"], pw["g_gla_o"], **dims)
        h = _merge(h, mod, layer, pw["g_mix"], o_a, o_p, o_g, pw["w_gates"], pw["w_branch"],
                   pw["w_out"], **dims)
        h = _ffn(h, mod, layer, 2, g_ffn2[layer], ffn2_w1[layer].astype(BF16),
                 ffn2_w3[layer].astype(BF16), ffn2_w2[layer].astype(BF16), lat_only=last, **dims)
    return h.reshape(n_batch, l, d)
```

```python
import functools
import math

import jax
import jax.numpy as jnp
import numpy as np
from jax import lax
from jax.experimental import pallas as pl
from jax.experimental.pallas import tpu as pltpu

F32 = jnp.float32
BF16 = jnp.bfloat16

EPS = 1e-6
GRID_W = 64
N_MOD = 9
MLA_HEADS = 8
MLA_Q_RANK = 256
MLA_KV_RANK = 128
MLA_NOPE = 64
MLA_ROPE = 32
MLA_V = 64
MLA_QK = MLA_NOPE + MLA_ROPE
AXIS_DIM = MLA_ROPE // 2
ROPE_THETA = 10000.0
POOL_WINDOWS = (2, 4, 8, 16)
BRANCH_W = 512
POOL_GDIM = BRANCH_W // len(POOL_WINDOWS)
GLA_HEADS = 4
GLA_DK = 64
GLA_DV = 128
GLA_GATE_RANK = 16
GLA_TAU = 16.0
GLA_CHUNK = 64
GLA_UNROLL = 4
N_BRANCH = 3
HEAD_LANES = 128

C_CQ, C_CKV, C_KR, C_PZ, C_GQ, C_GK, C_GV, C_GA, C_GR, C_END = (
    0, 256, 384, 512, 1024, 1280, 1536, 2048, 2176, 2688)

VMEM_LIMIT = 56 * 1024 * 1024


def _cparams(sem):
    return pltpu.CompilerParams(dimension_semantics=sem, vmem_limit_bytes=VMEM_LIMIT)


def _const_spec(shape):
    nd = len(shape)
    return pl.BlockSpec(shape, lambda *_: (0,) * nd, pipeline_mode=pl.Buffered(1))


def _dot(a, b):
    return jnp.dot(a, b, preferred_element_type=F32)


def _dot_nt(a, b):
    return lax.dot_general(a, b, (((1,), (1,)), ((), ())), preferred_element_type=F32)


def _dot_tn(a, b):
    return lax.dot_general(a, b, (((0,), (0,)), ((), ())), preferred_element_type=F32)


def _silu(x):
    return x * (1.0 / (1.0 + jnp.exp(-x)))


def _sigmoid(x):
    return 1.0 / (1.0 + jnp.exp(-x))


def _rms(x, g):
    ms = jnp.mean(x * x, axis=-1, keepdims=True)
    return x * lax.rsqrt(ms + EPS) * g


def _ada_kernel(c_ref, w_ref, b_ref, o_ref):
    s = _silu(c_ref[...]).astype(BF16)
    o_ref[...] = _dot(s, w_ref[...].astype(BF16)) + b_ref[...]


def _ada_mod(cond, w_ada, b_ada):
    depth, d, n = w_ada.shape
    rows = cond.shape[0]
    tn = 1024
    return pl.pallas_call(
        _ada_kernel,
        grid=(depth, n // tn),
        in_specs=[
            pl.BlockSpec((rows, d), lambda l, j: (0, 0)),
            pl.BlockSpec((None, d, tn), lambda l, j: (l, 0, j)),
            pl.BlockSpec((None, 1, tn), lambda l, j: (l, 0, j)),
        ],
        out_specs=pl.BlockSpec((None, rows, tn), lambda l, j: (l, 0, j)),
        out_shape=jax.ShapeDtypeStruct((depth, rows, n), F32),
        compiler_params=_cparams(("arbitrary", "arbitrary")),
        name="ada_mod",
    )(cond, w_ada, b_ada.reshape(depth, 1, n))


def _mod_rows(modb_ref, modc_ref, tile_in_batch, tm, d, lc):
    modb = modb_ref[...]
    modc = modc_ref[...]
    if lc == 0:
        return modb[0:1], modb[1:2], modb[2:3]
    row = tile_in_batch * tm + lax.broadcasted_iota(jnp.int32, (tm, d), 0)
    is_ctx = row < lc
    return tuple(jnp.where(is_ctx, modc[i:i + 1], modb[i:i + 1]) for i in range(3))


def _mod_specs(mod, layer, group, tiles_per_batch, n_batch):
    d = mod.shape[-1]
    blk = (None, None, None, 3, d)
    return [
        pl.BlockSpec(blk, lambda i: (layer, i // tiles_per_batch, group, 0, 0)),
        pl.BlockSpec(blk, lambda i: (layer, n_batch, group, 0, 0)),
    ]


def _ffn_kernel(h_ref, modb_ref, modc_ref, g_ref, w1_ref, w3_ref, w2_ref, o_ref, hid_ref,
                *, tm, lc, tiles_per_batch, col_chunk):
    d = h_ref.shape[-1]
    j = pl.program_id(0) % tiles_per_batch
    shift, scale, gate = _mod_rows(modb_ref, modc_ref, j, tm, d, lc)
    h = h_ref[...]
    u = (_rms(h, g_ref[...]) * (1.0 + scale) + shift).astype(BF16)
    d_ff = w1_ref.shape[-1]
    for c in range(d_ff // col_chunk):
        cols = slice(c * col_chunk, (c + 1) * col_chunk)
        a = _dot(u, w1_ref[:, cols])
        b = _dot(u, w3_ref[:, cols])
        hid_ref[:, cols] = (_silu(a) * b).astype(BF16)
    y = _dot(hid_ref[...], w2_ref[...])
    o_ref[...] = h + 0.5 * gate * y


def _ffn(h, mod, layer, group, g, w1, w3, w2, *, n_batch, lc, l, lat_only=False):
    t_all, d = h.shape
    d_ff = w1.shape[-1]
    r = lc + l
    if lat_only:
        tm = math.gcd(lc, l) if lc else l
        tm = min(tm, 256)
        tiles_per_batch = l // tm
        skip = lc // tm
        rows_out = n_batch * l
        in_map = lambda i: ((i // tiles_per_batch) * (r // tm) + skip + i % tiles_per_batch, 0)
        lc_k = 0
    else:
        tm = r // 8 if (r % 128 == 0 and r // 8 >= 128) else r // 2
        tiles_per_batch = r // tm
        rows_out = t_all
        in_map = lambda i: (i, 0)
        lc_k = lc
    col_chunk = d_ff // 2 if (d_ff // 2) % 128 == 0 else d_ff
    kern = functools.partial(_ffn_kernel, tm=tm, lc=lc_k, tiles_per_batch=tiles_per_batch,
                             col_chunk=col_chunk)
    return pl.pallas_call(
        kern,
        grid=(rows_out // tm,),
        in_specs=[pl.BlockSpec((tm, d), in_map)]
        + _mod_specs(mod, layer, group, tiles_per_batch, n_batch)
        + [_const_spec((1, d)), _const_spec((d, d_ff)), _const_spec((d, d_ff)),
           _const_spec((d_ff, d))],
        out_specs=pl.BlockSpec((tm, d), lambda i: (i, 0)),
        out_shape=jax.ShapeDtypeStruct((rows_out, d), F32),
        scratch_shapes=[pltpu.VMEM((tm, d_ff), BF16)],
        compiler_params=_cparams(("arbitrary",)),
        name="ffn_half_step",
    )(h, mod, mod, g.reshape(1, d), w1, w3, w2)


def _head_norm_rope(x, gain, tab, lane):
    ss = jnp.sum(jnp.where(lane < MLA_QK, x * x, 0.0), axis=-1, keepdims=True) * (1.0 / MLA_QK)
    t = x * lax.rsqrt(ss + EPS) * gain * tab
    moved = pltpu.roll(t, HEAD_LANES - MLA_ROPE, axis=1)
    return jnp.where(lane < MLA_NOPE, t, jnp.where(lane < MLA_QK, t + moved, 0.0))


def _mixin_kernel(h_ref, modb_ref, modc_ref, g_ref, w_ref, gcq_ref, wuq_ref, gckv_ref,
                  wuk_ref, wuv_ref, gq_ref, gk_ref, tab_ref,
                  q_out, k_out, v_out, pz_out, gq_out, gk_out, gv_out, ga_out, gr_out,
                  *, tm, lc, tiles_per_batch):
    d = h_ref.shape[-1]
    j = pl.program_id(0) % tiles_per_batch
    shift, scale, _ = _mod_rows(modb_ref, modc_ref, j, tm, d, lc)
    u = (_rms(h_ref[...], g_ref[...]) * (1.0 + scale) + shift).astype(BF16)

    pz_out[...] = _dot(u, w_ref[:, C_PZ:C_GQ])
    gq_out[...] = _dot(u, w_ref[:, C_GQ:C_GK])
    gk_out[...] = _dot(u, w_ref[:, C_GK:C_GV])
    gv_out[...] = _dot(u, w_ref[:, C_GV:C_GA]).astype(BF16)
    ga_out[...] = _dot(u, w_ref[:, C_GA:C_GR]).astype(BF16)
    gr_out[...] = _dot(u, w_ref[:, C_GR:C_END])

    lowrank = _dot(u, w_ref[:, C_CQ:C_PZ])
    cqn = _rms(lowrank[:, C_CQ:C_CKV], gcq_ref[...]).astype(BF16)
    ckvn = _rms(lowrank[:, C_CKV:C_KR], gckv_ref[...]).astype(BF16)
    kr = lowrank[:, C_KR:C_PZ]
    q_pre = _dot(cqn, wuq_ref[...])
    k_pre = _dot(ckvn, wuk_ref[...])
    v_out[...] = _dot(ckvn, wuv_ref[...]).astype(BF16)

    tab = tab_ref[...]
    lane = lax.broadcasted_iota(jnp.int32, (tm, HEAD_LANES), 1)
    gq = gq_ref[...]
    gk = gk_ref[...]
    for hd in range(MLA_HEADS):
        cols = slice(hd * HEAD_LANES, (hd + 1) * HEAD_LANES)
        q_out[:, cols] = _head_norm_rope(q_pre[:, cols], gq, tab, lane).astype(BF16)
        k_out[:, cols] = _head_norm_rope(k_pre[:, cols] + kr, gk, tab, lane).astype(BF16)


def _mixin(h, mod, layer, pw, tab, *, n_batch, lc, l):
    t_all, d = h.shape
    r = lc + l
    tm = r // 8 if (r % 128 == 0 and r // 8 >= 128) else r // 2
    tiles_per_batch = r // tm
    kern = functools.partial(_mixin_kernel, tm=tm, lc=lc, tiles_per_batch=tiles_per_batch)
    tok = lambda w: pl.BlockSpec((tm, w), lambda i: (i, 0))
    outs = [(MLA_HEADS * HEAD_LANES, BF16), (MLA_HEADS * HEAD_LANES, BF16),
            (MLA_HEADS * MLA_V, BF16), (BRANCH_W, F32), (GLA_HEADS * GLA_DK, F32),
            (GLA_HEADS * GLA_DK, F32), (GLA_HEADS * GLA_DV, BF16), (128, BF16),
            (GLA_HEADS * GLA_DV, F32)]
    return pl.pallas_call(
        kern,
        grid=(t_all // tm,),
        in_specs=[tok(d)] + _mod_specs(mod, layer, 1, tiles_per_batch, n_batch) + [
            _const_spec((1, d)), _const_spec((d, C_END)),
            _const_spec((1, MLA_Q_RANK)), _const_spec((MLA_Q_RANK, MLA_HEADS * HEAD_LANES)),
            _const_spec((1, MLA_KV_RANK)), _const_spec((MLA_KV_RANK, MLA_HEADS * HEAD_LANES)),
            _const_spec((MLA_KV_RANK, MLA_HEADS * MLA_V)),
            _const_spec((1, HEAD_LANES)), _const_spec((1, HEAD_LANES)),
            pl.BlockSpec((tm, HEAD_LANES), lambda i: (i % tiles_per_batch, 0)),
        ],
        out_specs=[tok(w) for w, _ in outs],
        out_shape=[jax.ShapeDtypeStruct((t_all, w), dt) for w, dt in outs],
        compiler_params=_cparams(("arbitrary",)),
        name="mixer_in_proj",
    )(h, mod, mod, pw["g_mix"], pw["w_all"], pw["g_cq"], pw["w_uq"], pw["g_ckv"],
      pw["w_uk"], pw["w_uv"], pw["gq_tab"], pw["gk_tab"], tab)


ONES_ROWS = 16
KEY_CHUNK = 256
SCORE_LOOKAHEAD = 5


def _attn_tile(q_ref, k_ref, vt_ref, o_ref, n_keys):
    tq = q_ref.shape[0]
    q = q_ref[...]
    lane = lax.broadcasted_iota(jnp.int32, q.shape, 1)
    zero = jnp.zeros_like(q)
    qbd = jnp.concatenate([jnp.where(lane < HEAD_LANES, q, zero),
                           jnp.where(lane < HEAD_LANES, zero, q)], axis=0)
    chunks = [(k0, min(KEY_CHUNK, n_keys - k0)) for k0 in range(0, n_keys, KEY_CHUNK)]

    def scores(k0, kn):
        return _dot_nt(k_ref[k0:k0 + kn, :], qbd)

    pending = [scores(*ch) for ch in chunks[:SCORE_LOOKAHEAD]]
    m = acc = None
    for ci, (k0, kn) in enumerate(chunks):
        if ci + SCORE_LOOKAHEAD < len(chunks):
            pending.append(scores(*chunks[ci + SCORE_LOOKAHEAD]))
        st = pending.pop(0)
        vt = jnp.concatenate([vt_ref[:, k0:k0 + kn], jnp.ones((ONES_ROWS, kn), BF16)], axis=0)
        cm = jnp.max(st, axis=0, keepdims=True)
        if m is None:
            m = cm
            acc = _dot(vt, jnp.exp2(st - m).astype(BF16))
        else:
            m_new = jnp.maximum(m, cm)
            acc = acc * jnp.exp2(m - m_new) + _dot(vt, jnp.exp2(st - m_new).astype(BF16))
            m = m_new
    inv =1.0 / acc[2 * MLA_V:2 * MLA_V + 1]
    o_t = jnp.concatenate([acc[0:MLA_V, 0:tq] * inv[:, 0:tq],
                           acc[MLA_V:2 * MLA_V, tq:] * inv[:, tq:]], axis=0)
    o_ref[...] = o_t.T.astype(o_ref.dtype)


def _attn_kernel(q_ref, k_ref, v_ref, o_ref, vt_ref, *, lc):
    r = k_ref.shape[0]
    i = pl.program_id(2)

    @pl.when(i == 0)
    def _():
        vt_ref[...] = v_ref[...].T

    if lc:
        @pl.when(i == 0)
        def _():
            _attn_tile(q_ref, k_ref, vt_ref, o_ref, lc)

    @pl.when(i > 0 if lc else i >= 0)
    def _():
        _attn_tile(q_ref, k_ref, vt_ref, o_ref, r)


def _attention(q, k, v, *, n_batch, lc, l):
    t_all = q.shape[0]
    r = lc + l
    tq = lc
    tiles = r // tq
    kern = functools.partial(_attn_kernel, lc=lc)
    return pl.pallas_call(
        kern,
        grid=(n_batch, MLA_HEADS // 2, tiles),
        in_specs=[
            pl.BlockSpec((tq, 2 * HEAD_LANES), lambda b, hp, i: (b * tiles + i, hp)),
            pl.BlockSpec((r, 2 * HEAD_LANES), lambda b, hp, i: (b, hp)),
            pl.BlockSpec((r, 2 * MLA_V), lambda b, hp, i: (b, hp)),
        ],
        out_specs=pl.BlockSpec((tq, 2 * MLA_V), lambda b, hp, i: (b * tiles + i, hp)),
        out_shape=jax.ShapeDtypeStruct((t_all, MLA_HEADS * MLA_V), BF16),
        scratch_shapes=[pltpu.VMEM((2 * MLA_V, r), BF16)],
        compiler_params=_cparams(("arbitrary", "arbitrary", "arbitrary")),
        name="mla_attention",
    )(q, k, v)


def _pool_kernel(p_ref, w_ref, s_ref, o_ref, *, lc, l):
    r = lc + l
    t = lax.broadcasted_iota(jnp.int32, (r, POOL_GDIM), 0)
    in_lat = t >= lc
    pos = jnp.where(in_lat, t - lc, t)
    seg_len = jnp.where(in_lat, l, lc)

    def shifted(x, dlt):
        src = pos + dlt
        rolled = pltpu.roll(x, (-dlt) % r, axis=0)
        return jnp.where(src >= 0, jnp.where(src < seg_len, rolled, 0.0), 0.0)

    for g, w in enumerate(POOL_WINDOWS):
        half = w // 2
        cols = slice(g * POOL_GDIM, (g + 1) * POOL_GDIM)
        x = p_ref[:, cols]
        right = x
        left = x
        step = 1
        while step < half:
            right = right + shifted(right, step)
            left = left + shifted(left, -step)
            step *= 2
        win = shifted(left, -1) + right
        cnt = (jnp.minimum(pos + half, seg_len) - jnp.maximum(pos - half, 0)).astype(F32)
        pooled = (win / cnt - x).astype(BF16)
        o_ref[:, cols] = (_dot(pooled, w_ref[g]) * s_ref[:, cols]).astype(o_ref.dtype)


def _pool(pz, w_pool, pool_scale, *, n_batch, lc, l):
    t_all = pz.shape[0]
    r = lc + l
    kern = functools.partial(_pool_kernel, lc=lc, l=l)
    return pl.pallas_call(
        kern,
        grid=(n_batch,),
        in_specs=[pl.BlockSpec((r, BRANCH_W), lambda b: (b, 0)),
                  _const_spec(w_pool.shape), _const_spec((1, BRANCH_W))],
        out_specs=pl.BlockSpec((r, BRANCH_W), lambda b: (b, 0)),
        out_shape=jax.ShapeDtypeStruct((t_all, BRANCH_W), BF16),
        compiler_params=_cparams(("arbitrary",)),
        name="multiscale_pool",
    )(pz, w_pool, pool_scale)


def _gla_kernel(q_ref, k_ref, v_ref, ga_ref, gr_ref, wa_ref, ba_ref, go_ref, o_ref,
                qt_s, kts_s, ke_s, dec_s, vbd_s, og_s, *, lc, l, blk):
    r = lc + l
    c = GLA_CHUNK
    n_chunks = r // c
    n_ctx = lc // c
    cpb = blk // c
    w = 2 * GLA_DK
    bi_ = lax.broadcasted_iota(jnp.int32, (blk, blk), 0)
    bj_ = lax.broadcasted_iota(jnp.int32, (blk, blk), 1)
    same_chunk = (bi_ // c) == (bj_ // c)
    ones_m = jnp.where(same_chunk, 1.0, 0.0).astype(BF16)
    ind = jnp.where(lax.broadcasted_iota(jnp.int32, (blk, cpb * w), 0) // c
                    == lax.broadcasted_iota(jnp.int32, (blk, cpb * w), 1) // w, 1.0, 0.0).astype(BF16)
    head0_k = lax.broadcasted_iota(jnp.int32, (blk, w), 1) < GLA_DK
    head0_v = lax.broadcasted_iota(jnp.int32, (blk, 2 * GLA_DV), 1) < GLA_DV
    ci_ = lax.broadcasted_iota(jnp.int32, (c, w), 0)
    cj_ = lax.broadcasted_iota(jnp.int32, (c, w), 1) % c
    state_diag = ((lax.broadcasted_iota(jnp.int32, (w, 2 * GLA_DV), 0) < GLA_DK)
                  == (lax.broadcasted_iota(jnp.int32, (w, 2 * GLA_DV), 1) < GLA_DV))

    def split_store(dst, b, x0, x1):
        for cc in range(cpb):
            base = (b * cpb + cc) * 2 * c
            dst[pl.ds(pl.multiple_of(base, c), c), :] = x0[cc * c:(cc + 1) * c]
            dst[pl.ds(pl.multiple_of(base + c, c), c), :] = x1[cc * c:(cc + 1) * c]

    def vprep(b, carry):
        v = v_ref[pl.ds(pl.multiple_of(b * blk, blk), blk), :]
        zero = jnp.zeros_like(v)
        split_store(vbd_s, b, jnp.where(head0_v, v, zero), jnp.where(head0_v, zero, v))
        return carry

    lax.fori_loop(0, r // blk, vprep, 0)
    og_s[...] = jnp.zeros_like(og_s)

    dirs = (0, 1)
    tri_m = [jnp.where(same_chunk, jnp.where((bj_ <= bi_) if d == 0 else (bj_ >= bi_), 1.0, 0.0),
                       0.0).astype(BF16) for d in dirs]
    causal = [(cj_ <= ci_) if d == 0 else (cj_ >= ci_) for d in dirs]

    def prep(b, carry):
        rows = pl.ds(pl.multiple_of(b * blk, blk), blk)
        ga = ga_ref[rows, :]
        logit = [_dot(ga, wa_ref[d]) + ba_ref[d] for d in dirs]
        la = [(jnp.minimum(x, 0.0) - jnp.log(1.0 + jnp.exp(-jnp.abs(x)))) * (1.0 / GLA_TAU)
              for x in logit]
        hi = [x.astype(BF16) for x in la]
        hl = [jnp.concatenate([hi[d], (la[d] - hi[d].astype(F32)).astype(BF16)], axis=1)
              for d in dirs]
        cs = [_dot(tri_m[d], hl[d]) for d in dirs]
        ts = [_dot(ones_m, hl[d]) for d in dirs]
        totb = [_dot_tn(hl[d], ind) for d in dirs]
        q = q_ref[rows, :] * (GLA_DK ** -0.5)
        k = k_ref[rows, :]
        for d in dirs:
            bcum = cs[d][:, :w] + cs[d][:, w:]
            tot = ts[d][:, :w] + ts[d][:, w:]
            for cc in range(cpb):
                dec_s[d, b * cpb + cc] = jnp.exp(totb[d][:w, cc * w:(cc + 1) * w]
                                                 + totb[d][w:, cc * w:(cc + 1) * w])
            qt_s[d, rows, :] = (q * jnp.exp(bcum)).astype(BF16)
            ke_s[d, rows, :] = (k * jnp.exp(tot - bcum)).astype(BF16)
            kt = (k * jnp.exp(-bcum)).astype(BF16)
            zero = jnp.zeros_like(kt)
            split_store(kts_s.at[d], b, jnp.where(head0_k, kt, zero), jnp.where(head0_k, zero, kt))
        return carry

    lax.fori_loop(0, r // blk, prep, 0)

    def scan(t, sts):
        work = []
        for u in range(GLA_UNROLL):
            i = t * GLA_UNROLL + u
            for d in dirs:
                if d == 0:
                    n = i
                else:
                    n = jnp.where(i < n_ctx, n_ctx - 1 - i, n_chunks - 1 - (i - n_ctx))
                rows = pl.ds(pl.multiple_of(n * c, c), c)
                rows2 = pl.ds(pl.multiple_of(n * 2 * c, 2 * c), 2 * c)
                qc = qt_s[d, rows, :]
                a = jnp.where(causal[d], _dot_nt(qc, kts_s[d, rows2, :]), 0.0).astype(BF16)
                dst = jnp.where(state_diag, _dot_tn(ke_s[d, rows, :], v_ref[rows, :]), 0.0)
                work.append((d, n, rows, rows2, qc, a, dst))
        sts = list(sts)
        for d, n, rows, rows2, qc, a, dst in work:
            o = _dot(jnp.concatenate([a, qc], axis=1),
                     jnp.concatenate([vbd_s[rows2, :], sts[d].astype(BF16)], axis=0))
            og_s[rows, :] = og_s[rows, :] + o
            dec = dec_s[d, n]
            sts[d] = sts[d] * jnp.concatenate([dec, dec], axis=1) + dst
        return tuple(sts)

    zero_state = jnp.zeros((w, 2 * GLA_DV), F32)
    lax.fori_loop(0, n_chunks // GLA_UNROLL, scan, (zero_state, zero_state))

    for hh in range(2):
        vc = slice(hh * GLA_DV, (hh + 1) * GLA_DV)
        on = _rms(og_s[:, vc], go_ref[...])
        o_ref[:, vc] = (on * _silu(gr_ref[:, vc])).astype(o_ref.dtype)


def _gla(gq, gk, gv, ga, gr, wa, ba, g_o, *, n_batch, lc, l):
    t_all = gq.shape[0]
    r = lc + l
    blk = 256 if r % 256 == 0 else 128
    kern = functools.partial(_gla_kernel, lc=lc, l=l, blk=blk)
    seq = lambda w: pl.BlockSpec((r, w), lambda b, hp: (b, hp), pipeline_mode=pl.Buffered(1))
    return pl.pallas_call(
        kern,
        grid=(n_batch, GLA_HEADS // 2),
        in_specs=[seq(2 * GLA_DK), seq(2 * GLA_DK), seq(2 * GLA_DV),
                  pl.BlockSpec((r, 128), lambda b, hp: (b, 0), pipeline_mode=pl.Buffered(1)),
                  seq(2 * GLA_DV),
                  pl.BlockSpec((2, 128, 2 * GLA_DK), lambda b, hp: (0, 0, hp)),
                  pl.BlockSpec((2, 1, 2 * GLA_DK), lambda b, hp: (0, 0, hp)),
                  pl.BlockSpec((1, GLA_DV), lambda b, hp: (0, 0))],
        out_specs=pl.BlockSpec((r, 2 * GLA_DV), lambda b, hp: (b, hp)),
        out_shape=jax.ShapeDtypeStruct((t_all, GLA_HEADS * GLA_DV), BF16),
        scratch_shapes=[pltpu.VMEM((2, r, 2 * GLA_DK), BF16), pltpu.VMEM((2, 2 * r, 2 * GLA_DK), BF16),
                        pltpu.VMEM((2, r, 2 * GLA_DK), BF16),
                        pltpu.VMEM((2, r // GLA_CHUNK, 2 * GLA_DK, 2 * GLA_DK), F32),
                        pltpu.VMEM((2 * r, 2 * GLA_DV), BF16), pltpu.VMEM((r, 2 * GLA_DV), F32)],
        compiler_params=_cparams(("arbitrary", "arbitrary")),
        name="gla_bidir",
    )(gq, gk, gv, ga, gr, wa, ba, g_o)


def _merge_kernel(h_ref, modb_ref, modc_ref, g_ref, oa_ref, op_ref, og_ref, wg_ref, wb_ref,
                  wo_ref, o_ref, *, tm, lc, tiles_per_batch):
    d = h_ref.shape[-1]
    j = pl.program_id(0) % tiles_per_batch
    shift, scale, gate = _mod_rows(modb_ref, modc_ref, j, tm, d, lc)
    h = h_ref[...]
    u = (_rms(h, g_ref[...]) * (1.0 + scale) + shift).astype(BF16)
    m = None
    for br, src in enumerate((oa_ref, op_ref, og_ref)):
        gt = _sigmoid(_dot(u, wg_ref[:, br * d:(br + 1) * d]))
        term = gt * _dot(src[...], wb_ref[br])
        m = term if m is None else m + term
    y = _dot(m.astype(BF16), wo_ref[...])
    o_ref[...] = h + gate * y


def _merge(h, mod, layer, g_mix, o_a, o_p, o_g, w_gates, w_branch, w_out, *, n_batch, lc, l):
    t_all, d = h.shape
    r = lc + l
    tm = r // 8 if (r % 128 == 0 and r // 8 >= 128) else r // 2
    tiles_per_batch = r // tm
    kern = functools.partial(_merge_kernel, tm=tm, lc=lc, tiles_per_batch=tiles_per_batch)
    tok = lambda w: pl.BlockSpec((tm, w), lambda i: (i, 0))
    return pl.pallas_call(
        kern,
        grid=(t_all // tm,),
        in_specs=[tok(d)] + _mod_specs(mod, layer, 1, tiles_per_batch, n_batch) + [
            _const_spec((1, d)), tok(BRANCH_W), tok(BRANCH_W), tok(BRANCH_W),
            _const_spec(w_gates.shape), _const_spec(w_branch.shape), _const_spec(w_out.shape)],
        out_specs=tok(d),
        out_shape=jax.ShapeDtypeStruct((t_all, d), F32),
        compiler_params=_cparams(("arbitrary",)),
        name="branch_merge",
    )(h, mod, mod, g_mix, o_a, o_p, o_g, w_gates, w_branch, w_out)


_ROPE_PARTNER = np.arange(MLA_ROPE) ^ (AXIS_DIM // 2)


def _rope_table(lc, l):
    pos = jnp.arange(l)
    row = (pos // GRID_W).astype(F32)
    col = (pos % GRID_W).astype(F32)
    inv = ROPE_THETA ** (-jnp.arange(0, AXIS_DIM, 2, dtype=F32) / AXIS_DIM)
    ang = jnp.stack([row, col], axis=-1)[:, :, None] * inv
    cos = jnp.repeat(jnp.cos(ang)[:, :, None, :], 2, axis=2).reshape(l, MLA_ROPE)
    sin = jnp.repeat(jnp.sin(ang)[:, :, None, :], 2, axis=2)
    sin = (sin * jnp.array([-1.0, 1.0], F32)[None, None, :, None]).reshape(l, MLA_ROPE)
    lat = jnp.concatenate([jnp.ones((l, MLA_NOPE), F32), cos, sin], axis=1)
    ctx = jnp.concatenate([jnp.ones((lc, MLA_QK), F32), jnp.zeros((lc, MLA_ROPE), F32)], axis=1)
    return jnp.concatenate([ctx, lat], axis=0)


def _pack_layer(p, layer):
    d = p["w_in"].shape[1]
    w_in = p["w_in"][layer]
    offs = np.cumsum([0, MLA_Q_RANK, MLA_KV_RANK, MLA_ROPE, BRANCH_W, GLA_HEADS * GLA_DK,
                      GLA_HEADS * GLA_DK, GLA_HEADS * GLA_DV, 2 * GLA_GATE_RANK,
                      GLA_HEADS * GLA_DV, N_BRANCH * d])
    cq, ckv, kr, pz, gq, gk, gv, ga, gr, gates = [w_in[:, offs[i]:offs[i + 1]] for i in range(10)]
    kr_ext = jnp.concatenate([jnp.zeros((d, MLA_NOPE), F32), kr, kr[:, _ROPE_PARTNER]], axis=1)
    ga_ext = jnp.concatenate([ga, jnp.zeros((d, 128 - 2 * GLA_GATE_RANK), F32)], axis=1)
    w_all = jnp.concatenate([cq, ckv, kr_ext, pz, gq, gk, gv, ga_ext, gr], axis=1).astype(BF16)

    w_uq = p["w_uq"][layer].reshape(MLA_Q_RANK, MLA_HEADS, MLA_QK)
    w_uq = jnp.concatenate([w_uq, w_uq[:, :, MLA_NOPE + _ROPE_PARTNER]], axis=2)
    w_uq = w_uq.reshape(MLA_Q_RANK, MLA_HEADS * HEAD_LANES).astype(BF16)
    w_ukv = p["w_ukv"][layer].reshape(MLA_KV_RANK, MLA_HEADS, MLA_NOPE + MLA_V)
    w_uk = jnp.concatenate([w_ukv[:, :, :MLA_NOPE],
                            jnp.zeros((MLA_KV_RANK, MLA_HEADS, HEAD_LANES - MLA_NOPE), F32)], axis=2)
    w_uk = w_uk.reshape(MLA_KV_RANK, MLA_HEADS * HEAD_LANES).astype(BF16)
    w_uv = w_ukv[:, :, MLA_NOPE:].reshape(MLA_KV_RANK, MLA_HEADS * MLA_V).astype(BF16)

    def gain_tab(g):
        return jnp.concatenate([g, g[MLA_NOPE + _ROPE_PARTNER]])[None, :]

    wa = jnp.zeros((2, 128, GLA_HEADS * GLA_DK), F32)
    for dr in range(2):
        wa = wa.at[dr, dr * GLA_GATE_RANK:(dr + 1) * GLA_GATE_RANK].set(p["w_a2"][layer, dr])

    return dict(
        g_mix=p["g_mix"][layer][None, :], w_all=w_all,
        g_cq=p["g_cq"][layer][None, :], w_uq=w_uq,
        g_ckv=p["g_ckv"][layer][None, :], w_uk=w_uk, w_uv=w_uv,
        gq_tab=gain_tab(p["g_qn"][layer]) * (MLA_QK ** -0.5 * math.log2(math.e)), gk_tab=gain_tab(p["g_kn"][layer]),
        w_gates=gates.astype(BF16),
        w_pool=p["w_pool"][layer].astype(BF16), pool_scale=p["pool_scale"][layer][None, :],
        wa=wa.astype(BF16), ba=p["b_a2"][layer][:, None, :], g_gla_o=p["g_gla_o"][layer][None, :],
        w_branch=p["w_branch"][layer].astype(BF16), w_out=p["w_out"][layer].astype(BF16),
    )


def kernel(x, c, ctx, c_ctx, w_ada, b_ada, g_ffn1, ffn1_w1, ffn1_w3, ffn1_w2, g_mix, w_in,
           g_cq, w_uq, g_ckv, w_ukv, g_qn, g_kn, w_pool, pool_scale, w_a2, b_a2, g_gla_o,
           w_branch, w_out, g_ffn2, ffn2_w1, ffn2_w3, ffn2_w2):
    n_batch, l, d = x.shape
    lc = ctx.shape[1]
    depth = w_ada.shape[0]
    dims = dict(n_batch=n_batch, lc=lc, l=l)
    p = dict(w_in=w_in, g_mix=g_mix, g_cq=g_cq, w_uq=w_uq, g_ckv=g_ckv, w_ukv=w_ukv, g_qn=g_qn,
             g_kn=g_kn, w_pool=w_pool, pool_scale=pool_scale, w_a2=w_a2, b_a2=b_a2,
             g_gla_o=g_gla_o, w_branch=w_branch, w_out=w_out)

    cond_rows = 16
    cond = jnp.zeros((cond_rows, d), F32).at[:n_batch].set(c).at[n_batch].set(c_ctx)
    mod = _ada_mod(cond, w_ada, b_ada).reshape(depth, cond_rows, 3, 3, d)

    tab = _rope_table(lc, l)
    h = jnp.concatenate([ctx, x], axis=1).reshape(n_batch * (lc + l), d)
    for layer in range(depth):
        last = layer == depth - 1
        pw = _pack_layer(p, layer)
        h = _ffn(h, mod, layer, 0, g_ffn1[layer], ffn1_w1[layer].astype(BF16),
                 ffn1_w3[layer].astype(BF16), ffn1_w2[layer].astype(BF16), **dims)
        q, k, v, pz, gq, gk, gv, ga, gr = _mixin(h, mod, layer, pw, tab, **dims)
        o_a = _attention(q, k, v, **dims)
        o_p = _pool(pz, pw["w_pool"], pw["pool_scale"], **dims)
        o_g = _gla(gq, gk, gv, ga, gr, pw["wa"], pw["ba"], pw["g_gla_o"], **dims)
        h = _merge(h, mod, layer, pw["g_mix"], o_a, o_p, o_g, pw["w_gates"], pw["w_branch"],
                   pw["w_out"], **dims)
        h = _ffn(h, mod, layer, 2, g_ffn2[layer], ffn2_w1[layer].astype(BF16),
                 ffn2_w3[layer].astype(BF16), ffn2_w2[layer].astype(BF16), lat_only=last, **dims)
    return h.reshape(n_batch, l, d)
```

```python
import functools
import math

import jax
import jax.numpy as jnp
import numpy as np
from jax import lax
from jax.experimental import pallas as pl
from jax.experimental.pallas import tpu as pltpu

F32 = jnp.float32
BF16 = jnp.bfloat16

EPS = 1e-6
GRID_W = 64
N_MOD = 9
MLA_HEADS = 8
MLA_Q_RANK = 256
MLA_KV_RANK = 128
MLA_NOPE = 64
MLA_ROPE = 32
MLA_V = 64
MLA_QK = MLA_NOPE + MLA_ROPE
AXIS_DIM = MLA_ROPE // 2
ROPE_THETA = 10000.0
POOL_WINDOWS = (2, 4, 8, 16)
BRANCH_W = 512
POOL_GDIM = BRANCH_W // len(POOL_WINDOWS)
GLA_HEADS = 4
GLA_DK = 64
GLA_DV = 128
GLA_GATE_RANK = 16
GLA_TAU = 16.0
GLA_CHUNK = 64
GLA_UNROLL = 4
N_BRANCH = 3
HEAD_LANES = 128

C_CQ, C_CKV, C_KRA, C_KRB, C_PZ, C_GQ, C_GK, C_GV, C_GA, C_GR, C_END = (
    0, 256, 384, 512, 640, 1152, 1408, 1664, 2176, 2304, 2816)

VMEM_LIMIT = 56 * 1024 * 1024


def _cparams(sem):
    return pltpu.CompilerParams(dimension_semantics=sem, vmem_limit_bytes=VMEM_LIMIT)


def _const_spec(shape):
    nd = len(shape)
    return pl.BlockSpec(shape, lambda *_: (0,) * nd, pipeline_mode=pl.Buffered(1))


def _dot(a, b):
    return jnp.dot(a, b, preferred_element_type=F32)


def _dot_nt(a, b):
    return lax.dot_general(a, b, (((1,), (1,)), ((), ())), preferred_element_type=F32)


def _dot_tn(a, b):
    return lax.dot_general(a, b, (((0,), (0,)), ((), ())), preferred_element_type=F32)


def _silu(x):
    return x * (1.0 / (1.0 + jnp.exp(-x)))


def _sigmoid(x):
    return 1.0 / (1.0 + jnp.exp(-x))


def _rms(x, g):
    ms = jnp.mean(x * x, axis=-1, keepdims=True)
    return x * lax.rsqrt(ms + EPS) * g


def _ada_kernel(c_ref, w_ref, b_ref, o_ref):
    s = _silu(c_ref[...]).astype(BF16)
    o_ref[...] = _dot(s, w_ref[...].astype(BF16)) + b_ref[...]


def _ada_mod(cond, w_ada, b_ada):
    depth, d, n = w_ada.shape
    rows = cond.shape[0]
    tn = 1024
    return pl.pallas_call(
        _ada_kernel,
        grid=(depth, n // tn),
        in_specs=[
            pl.BlockSpec((rows, d), lambda l, j: (0, 0)),
            pl.BlockSpec((None, d, tn), lambda l, j: (l, 0, j)),
            pl.BlockSpec((None, 1, tn), lambda l, j: (l, 0, j)),
        ],
        out_specs=pl.BlockSpec((None, rows, tn), lambda l, j: (l, 0, j)),
        out_shape=jax.ShapeDtypeStruct((depth, rows, n), F32),
        compiler_params=_cparams(("arbitrary", "arbitrary")),
        name="ada_mod",
    )(cond, w_ada, b_ada.reshape(depth, 1, n))


def _mod_rows(modb_ref, modc_ref, tile_in_batch, tm, d, lc):
    modb = modb_ref[...]
    modc = modc_ref[...]
    if lc == 0:
        return modb[0:1], modb[1:2], modb[2:3]
    row = tile_in_batch * tm + lax.broadcasted_iota(jnp.int32, (tm, d), 0)
    is_ctx = row < lc
    return tuple(jnp.where(is_ctx, modc[i:i + 1], modb[i:i + 1]) for i in range(3))


def _mod_specs(mod, layer, group, tiles_per_batch, n_batch):
    d = mod.shape[-1]
    blk = (None, None, None, 3, d)
    return [
        pl.BlockSpec(blk, lambda i: (layer, i // tiles_per_batch, group, 0, 0)),
        pl.BlockSpec(blk, lambda i: (layer, n_batch, group, 0, 0)),
    ]


def _ffn_kernel(h_ref, modb_ref, modc_ref, g_ref, w1_ref, w3_ref, w2_ref, o_ref, hid_ref,
                *, tm, lc, tiles_per_batch, col_chunk):
    d = h_ref.shape[-1]
    j = pl.program_id(0) % tiles_per_batch
    shift, scale, gate = _mod_rows(modb_ref, modc_ref, j, tm, d, lc)
    h = h_ref[...]
    u = (_rms(h, g_ref[...]) * (1.0 + scale) + shift).astype(BF16)
    d_ff = w1_ref.shape[-1]
    for c0 in range(0, d_ff, col_chunk):
        cols = slice(c0, min(c0 + col_chunk, d_ff))
        a = _dot(u, w1_ref[:, cols])
        b = _dot(u, w3_ref[:, cols])
        hid_ref[:, cols] = (_silu(a) * b).astype(BF16)
    y = _dot(hid_ref[...], w2_ref[...])
    o_ref[...] = h + 0.5 * gate * y


def _ffn(h, mod, layer, group, g, w1, w3, w2, *, n_batch, lc, l, lat_only=False):
    t_all, d = h.shape
    d_ff = w1.shape[-1]
    r = lc + l
    if lat_only:
        tm = math.gcd(lc, l) if lc else l
        tm = min(tm, 256)
        tiles_per_batch = l // tm
        skip = lc // tm
        rows_out = n_batch * l
        in_map = lambda i: ((i // tiles_per_batch) * (r // tm) + skip + i % tiles_per_batch, 0)
        lc_k = 0
    else:
        tm = r // 4
        tiles_per_batch = r // tm
        rows_out = t_all
        in_map = lambda i: (i, 0)
        lc_k = lc
    col_chunk = min(d_ff, 512)
    kern = functools.partial(_ffn_kernel, tm=tm, lc=lc_k, tiles_per_batch=tiles_per_batch,
                             col_chunk=col_chunk)
    return pl.pallas_call(
        kern,
        grid=(rows_out // tm,),
        in_specs=[pl.BlockSpec((tm, d), in_map)]
        + _mod_specs(mod, layer, group, tiles_per_batch, n_batch)
        + [_const_spec((1, d)), _const_spec((d, d_ff)), _const_spec((d, d_ff)),
           _const_spec((d_ff, d))],
        out_specs=pl.BlockSpec((tm, d), lambda i: (i, 0)),
        out_shape=jax.ShapeDtypeStruct((rows_out, d), F32),
        scratch_shapes=[pltpu.VMEM((tm, d_ff), BF16)],
        compiler_params=_cparams(("arbitrary",)),
        name="ffn_half_step",
    )(h, mod, mod, g.reshape(1, d), w1, w3, w2)


def _head_norm_rope(xa, rot, tab_a):
    ss = jnp.sum(xa * xa, axis=-1, keepdims=True) * (1.0 / MLA_QK)
    return ((xa * tab_a + rot) * lax.rsqrt(ss + EPS)).astype(BF16)


def _mixin_kernel(h_ref, modb_ref, modc_ref, g_ref, w_ref, gcq_ref, wuq_ref, gckv_ref,
                  wuk_ref, wuv_ref, gq_ref, gk_ref, tabc_ref, tabs_ref,
                  q_out, k_out, v_out, pz_out, gq_out, gk_out, gv_out, ga_out, gr_out,
                  *, tm, lc, tiles_per_batch):
    d = h_ref.shape[-1]
    j = pl.program_id(0) % tiles_per_batch
    shift, scale, _ = _mod_rows(modb_ref, modc_ref, j, tm, d, lc)
    u = (_rms(h_ref[...], g_ref[...]) * (1.0 + scale) + shift).astype(BF16)

    lowrank = _dot(u, w_ref[:, C_CQ:C_PZ])
    cqn = _rms(lowrank[:, C_CQ:C_CKV], gcq_ref[...]).astype(BF16)
    ckvn = _rms(lowrank[:, C_CKV:C_KRA], gckv_ref[...]).astype(BF16)
    kr_a = lowrank[:, C_KRA:C_KRB]
    kr_b = lowrank[:, C_KRB:C_PZ]
    q_pre = _dot(cqn, wuq_ref[...])
    k_pre = _dot(ckvn, wuk_ref[...])
    v_out[...] = _dot(ckvn, wuv_ref[...]).astype(BF16)

    tab_c = tabc_ref[...]
    tab_s = tabs_ref[...]
    gq = gq_ref[...]
    gk = gk_ref[...]
    qa_tab, qb_tab = tab_c * gq[0:1], tab_s * gq[1:2]
    ka_tab = tab_c * gk[0:1]
    k_rot = kr_b * (tab_s * gk[1:2])
    n_q = MLA_HEADS * HEAD_LANES

    def other(idx):
        if idx == 0:
            pz_out[...] = _dot(u, w_ref[:, C_PZ:C_GQ])
        elif idx == 1:
            gq_out[...] = _dot(u, w_ref[:, C_GQ:C_GK])
        elif idx == 2:
            gk_out[...] = _dot(u, w_ref[:, C_GK:C_GV])
        elif idx == 3:
            gv_out[...] = _dot(u, w_ref[:, C_GV:C_GA]).astype(BF16)
        elif idx == 4:
            ga_out[...] = _dot(u, w_ref[:, C_GA:C_GR]).astype(BF16)
        elif idx == 5:
            gr_out[...] = _dot(u, w_ref[:, C_GR:C_END])

    for hd in range(MLA_HEADS):
        cols = slice(hd * HEAD_LANES, (hd + 1) * HEAD_LANES)
        cols_b = slice(n_q + hd * HEAD_LANES, n_q + (hd + 1) * HEAD_LANES)
        pair, half = hd // 2, slice((hd % 2) * HEAD_LANES, (hd % 2 + 1) * HEAD_LANES)
        q_out[pair, :, half] = _head_norm_rope(q_pre[:, cols], q_pre[:, cols_b] * qb_tab, qa_tab)
        k_out[pair, :, half] = _head_norm_rope(k_pre[:, cols] + kr_a, k_rot, ka_tab)
        other(hd)


def _mixin(h, mod, layer, pw, tab, *, n_batch, lc, l):
    t_all, d = h.shape
    r = lc + l
    tm = r // 8 if (r % 128 == 0 and r // 8 >= 128) else r // 2
    tiles_per_batch = r // tm
    kern = functools.partial(_mixin_kernel, tm=tm, lc=lc, tiles_per_batch=tiles_per_batch)
    tok = lambda w: pl.BlockSpec((tm, w), lambda i: (i, 0))
    n_pairs = MLA_HEADS // 2
    qk_spec = pl.BlockSpec((n_pairs, tm, 2 * HEAD_LANES), lambda i: (0, i, 0))
    qk_shape = jax.ShapeDtypeStruct((n_pairs, t_all, 2 * HEAD_LANES), BF16)
    outs = [(MLA_HEADS * MLA_V, BF16), (BRANCH_W, F32), (GLA_HEADS * GLA_DK, F32),
            (GLA_HEADS * GLA_DK, F32), (GLA_HEADS * GLA_DV, BF16), (128, BF16),
            (GLA_HEADS * GLA_DV, F32)]
    return pl.pallas_call(
        kern,
        grid=(t_all // tm,),
        in_specs=[tok(d)] + _mod_specs(mod, layer, 1, tiles_per_batch, n_batch) + [
            _const_spec((1, d)), _const_spec((d, C_END)),
            _const_spec((1, MLA_Q_RANK)), _const_spec((MLA_Q_RANK, 2 * MLA_HEADS * HEAD_LANES)),
            _const_spec((1, MLA_KV_RANK)), _const_spec((MLA_KV_RANK, MLA_HEADS * HEAD_LANES)),
            _const_spec((MLA_KV_RANK, MLA_HEADS * MLA_V)),
            _const_spec((2, HEAD_LANES)), _const_spec((2, HEAD_LANES)),
            pl.BlockSpec((tm, HEAD_LANES), lambda i: (i % tiles_per_batch, 0)),
            pl.BlockSpec((tm, HEAD_LANES), lambda i: (i % tiles_per_batch, 0)),
        ],
        out_specs=[qk_spec, qk_spec] + [tok(w) for w, _ in outs],
        out_shape=[qk_shape, qk_shape] + [jax.ShapeDtypeStruct((t_all, w), dt) for w, dt in outs],
        compiler_params=_cparams(("arbitrary",)),
        name="mixer_in_proj",
    )(h, mod, mod, pw["g_mix"], pw["w_all"], pw["g_cq"], pw["w_uq"], pw["g_ckv"],
      pw["w_uk"], pw["w_uv"], pw["gq_tab"], pw["gk_tab"], tab[0], tab[1])


ONES_ROWS = 16
KEY_CHUNK = 256
SCORE_LOOKAHEAD = 5


def _attn_tile(q_ref, k_ref, vt_ref, o_ref, hp, n_keys):
    tq = q_ref.shape[1]
    q = q_ref[hp]
    vrows = slice(hp * 2 * MLA_V, (hp + 1) * 2 * MLA_V)
    lane = lax.broadcasted_iota(jnp.int32, q.shape, 1)
    zero = jnp.zeros_like(q)
    qbd = jnp.concatenate([jnp.where(lane < HEAD_LANES, q, zero),
                           jnp.where(lane < HEAD_LANES, zero, q)], axis=0)
    chunks = [(k0, min(KEY_CHUNK, n_keys - k0)) for k0 in range(0, n_keys, KEY_CHUNK)]

    def scores(k0, kn):
        return _dot_nt(k_ref[hp, k0:k0 + kn, :], qbd)

    pending = [scores(*ch) for ch in chunks[:SCORE_LOOKAHEAD]]
    m = acc = None
    for ci, (k0, kn) in enumerate(chunks):
        if ci + SCORE_LOOKAHEAD < len(chunks):
            pending.append(scores(*chunks[ci + SCORE_LOOKAHEAD]))
        st = pending.pop(0)
        vt = jnp.concatenate([vt_ref[vrows, k0:k0 + kn], jnp.ones((ONES_ROWS, kn), BF16)], axis=0)
        cm = jnp.max(st, axis=0, keepdims=True)
        if m is None:
            m = cm
            acc = _dot(vt, jnp.exp2(st - m).astype(BF16))
        else:
            m_new = jnp.maximum(m, cm)
            acc = acc * jnp.exp2(m - m_new) + _dot(vt, jnp.exp2(st - m_new).astype(BF16))
            m = m_new
    inv = 1.0 / acc[2 * MLA_V:2 * MLA_V + 1]
    o_t = jnp.concatenate([acc[0:MLA_V, 0:tq] * inv[:, 0:tq],
                           acc[MLA_V:2 * MLA_V, tq:] * inv[:, tq:]], axis=0)
    o_ref[:, vrows] = o_t.T.astype(o_ref.dtype)


def _attn_kernel(q_ref, k_ref, v_ref, o_ref, vt_ref, *, lc):
    r = k_ref.shape[1]
    i = pl.program_id(1)
    pairs = range(MLA_HEADS // 2)

    @pl.when(i == 0)
    def _():
        vt_ref[...] = v_ref[...].T

    if lc:
        @pl.when(i == 0)
        def _():
            for hp in pairs:
                _attn_tile(q_ref, k_ref, vt_ref, o_ref, hp, lc)

    @pl.when(i > 0 if lc else i >= 0)
    def _():
        for hp in pairs:
            _attn_tile(q_ref, k_ref, vt_ref, o_ref, hp, r)


def _attention(q, k, v, *, n_batch, lc, l):
    n_pairs, t_all, _ = q.shape
    r = lc + l
    tq = lc
    tiles = r // tq
    kern = functools.partial(_attn_kernel, lc=lc)
    return pl.pallas_call(
        kern,
        grid=(n_batch, tiles),
        in_specs=[
            pl.BlockSpec((n_pairs, tq, 2 * HEAD_LANES), lambda b, i: (0, b * tiles + i, 0)),
            pl.BlockSpec((n_pairs, r, 2 * HEAD_LANES), lambda b, i: (0, b, 0)),
            pl.BlockSpec((r, MLA_HEADS * MLA_V), lambda b, i: (b, 0)),
        ],
        out_specs=pl.BlockSpec((tq, MLA_HEADS * MLA_V), lambda b, i: (b * tiles + i, 0)),
        out_shape=jax.ShapeDtypeStruct((t_all, MLA_HEADS * MLA_V), BF16),
        scratch_shapes=[pltpu.VMEM((MLA_HEADS * MLA_V, r), BF16)],
        compiler_params=_cparams(("arbitrary", "arbitrary")),
        name="mla_attention",
    )(q, k, v)


def _pool_kernel(p_ref, w_ref, s_ref, o_ref, *, lc, l):
    r = lc + l
    t = lax.broadcasted_iota(jnp.int32, (r, POOL_GDIM), 0)
    in_lat = t >= lc
    pos = jnp.where(in_lat, t - lc, t)
    seg_len = jnp.where(in_lat, l, lc)

    def shifted(x, dlt):
        src = pos + dlt
        rolled = pltpu.roll(x, (-dlt) % r, axis=0)
        return jnp.where(src >= 0, jnp.where(src < seg_len, rolled, 0.0), 0.0)

    for g, w in enumerate(POOL_WINDOWS):
        half = w // 2
        cols = slice(g * POOL_GDIM, (g + 1) * POOL_GDIM)
        x = p_ref[:, cols]
        right = x
        left = x
        step = 1
        while step < half:
            right = right + shifted(right, step)
            left = left + shifted(left, -step)
            step *= 2
        win = shifted(left, -1) + right
        cnt = (jnp.minimum(pos + half, seg_len) - jnp.maximum(pos - half, 0)).astype(F32)
        pooled = (win / cnt - x).astype(BF16)
        o_ref[:, cols] = (_dot(pooled, w_ref[g]) * s_ref[:, cols]).astype(o_ref.dtype)


def _pool(pz, w_pool, pool_scale, *, n_batch, lc, l):
    t_all = pz.shape[0]
    r = lc + l
    kern = functools.partial(_pool_kernel, lc=lc, l=l)
    return pl.pallas_call(
        kern,
        grid=(n_batch,),
        in_specs=[pl.BlockSpec((r, BRANCH_W), lambda b: (b, 0)),
                  _const_spec(w_pool.shape), _const_spec((1, BRANCH_W))],
        out_specs=pl.BlockSpec((r, BRANCH_W), lambda b: (b, 0)),
        out_shape=jax.ShapeDtypeStruct((t_all, BRANCH_W), BF16),
        compiler_params=_cparams(("arbitrary",)),
        name="multiscale_pool",
    )(pz, w_pool, pool_scale)


def _gla_kernel(q_ref, k_ref, v_ref, ga_ref, gr_ref, wa_ref, ba_ref, go_ref, o_ref,
                qt_s, kts_s, ke_s, dec_s, vbd_s, og_s, *, lc, l, blk):
    r = lc + l
    c = GLA_CHUNK
    n_chunks = r // c
    n_ctx = lc // c
    cpb = blk // c
    w = 2 * GLA_DK
    bi_ = lax.broadcasted_iota(jnp.int32, (blk, blk), 0)
    bj_ = lax.broadcasted_iota(jnp.int32, (blk, blk), 1)
    same_chunk = (bi_ // c) == (bj_ // c)
    ones_m = jnp.where(same_chunk, 1.0, 0.0).astype(BF16)
    ind = jnp.where(lax.broadcasted_iota(jnp.int32, (blk, cpb * w), 0) // c
                    == lax.broadcasted_iota(jnp.int32, (blk, cpb * w), 1) // w, 1.0, 0.0).astype(BF16)
    head0_k = lax.broadcasted_iota(jnp.int32, (blk, w), 1) < GLA_DK
    head0_v = lax.broadcasted_iota(jnp.int32, (blk, 2 * GLA_DV), 1) < GLA_DV
    ci_ = lax.broadcasted_iota(jnp.int32, (c, w), 0)
    cj_ = lax.broadcasted_iota(jnp.int32, (c, w), 1) % c
    state_diag = ((lax.broadcasted_iota(jnp.int32, (w, 2 * GLA_DV), 0) < GLA_DK)
                  == (lax.broadcasted_iota(jnp.int32, (w, 2 * GLA_DV), 1) < GLA_DV))

    def split_store(dst, b, x0, x1):
        for cc in range(cpb):
            base = (b * cpb + cc) * 2 * c
            dst[pl.ds(pl.multiple_of(base, c), c), :] = x0[cc * c:(cc + 1) * c]
            dst[pl.ds(pl.multiple_of(base + c, c), c), :] = x1[cc * c:(cc + 1) * c]

    def vprep(b, carry):
        v = v_ref[pl.ds(pl.multiple_of(b * blk, blk), blk), :]
        zero = jnp.zeros_like(v)
        split_store(vbd_s, b, jnp.where(head0_v, v, zero), jnp.where(head0_v, zero, v))
        return carry

    lax.fori_loop(0, r // blk, vprep, 0)
    og_s[...] = jnp.zeros_like(og_s)

    dirs = (0, 1)
    tri_m = [jnp.where(same_chunk, jnp.where((bj_ <= bi_) if d == 0 else (bj_ >= bi_), 1.0, 0.0),
                       0.0).astype(BF16) for d in dirs]
    causal = [(cj_ <= ci_) if d == 0 else (cj_ >= ci_) for d in dirs]

    def prep(b, carry):
        rows = pl.ds(pl.multiple_of(b * blk, blk), blk)
        ga = ga_ref[rows, :]
        logit = [_dot(ga, wa_ref[d]) + ba_ref[d] for d in dirs]
        la = [(jnp.minimum(x, 0.0) - jnp.log(1.0 + jnp.exp(-jnp.abs(x)))) * (1.0 / GLA_TAU)
              for x in logit]
        hi = [x.astype(BF16) for x in la]
        hl = [jnp.concatenate([hi[d], (la[d] - hi[d].astype(F32)).astype(BF16)], axis=1)
              for d in dirs]
        cs = [_dot(tri_m[d], hl[d]) for d in dirs]
        ts = [_dot(ones_m, hl[d]) for d in dirs]
        totb = [_dot_tn(hl[d], ind) for d in dirs]
        q = q_ref[rows, :] * (GLA_DK ** -0.5)
        k = k_ref[rows, :]
        for d in dirs:
            bcum = cs[d][:, :w] + cs[d][:, w:]
            tot = ts[d][:, :w] + ts[d][:, w:]
            for cc in range(cpb):
                dec_s[d, b * cpb + cc] = jnp.exp(totb[d][:w, cc * w:(cc + 1) * w]
                                                 + totb[d][w:, cc * w:(cc + 1) * w])
            qt_s[d, rows, :] = (q * jnp.exp(bcum)).astype(BF16)
            ke_s[d, rows, :] = (k * jnp.exp(tot - bcum)).astype(BF16)
            kt = (k * jnp.exp(-bcum)).astype(BF16)
            zero = jnp.zeros_like(kt)
            split_store(kts_s.at[d], b, jnp.where(head0_k, kt, zero), jnp.where(head0_k, zero, kt))
        return carry

    lax.fori_loop(0, r // blk, prep, 0)

    def scan(t, sts):
        work = []
        for u in range(GLA_UNROLL):
            i = t * GLA_UNROLL + u
            for d in dirs:
                if d == 0:
                    n = i
                else:
                    n = jnp.where(i < n_ctx, n_ctx - 1 - i, n_chunks - 1 - (i - n_ctx))
                rows = pl.ds(pl.multiple_of(n * c, c), c)
                rows2 = pl.ds(pl.multiple_of(n * 2 * c, 2 * c), 2 * c)
                qc = qt_s[d, rows, :]
                a = jnp.where(causal[d], _dot_nt(qc, kts_s[d, rows2, :]), 0.0).astype(BF16)
                dst = jnp.where(state_diag, _dot_tn(ke_s[d, rows, :], v_ref[rows, :]), 0.0)
                work.append((d, n, rows, rows2, qc, a, dst))
        sts = list(sts)
        for d, n, rows, rows2, qc, a, dst in work:
            o = _dot(jnp.concatenate([a, qc], axis=1),
                     jnp.concatenate([vbd_s[rows2, :], sts[d].astype(BF16)], axis=0))
            og_s[rows, :] = og_s[rows, :] + o
            dec = dec_s[d, n]
            sts[d] = sts[d] * jnp.concatenate([dec, dec], axis=1) + dst
        return tuple(sts)

    zero_state = jnp.zeros((w, 2 * GLA_DV), F32)
    lax.fori_loop(0, n_chunks // GLA_UNROLL, scan, (zero_state, zero_state))

    for hh in range(2):
        vc = slice(hh * GLA_DV, (hh + 1) * GLA_DV)
        on = _rms(og_s[:, vc], go_ref[...])
        o_ref[:, vc] = (on * _silu(gr_ref[:, vc])).astype(o_ref.dtype)


def _gla(gq, gk, gv, ga, gr, wa, ba, g_o, *, n_batch, lc, l):
    t_all = gq.shape[0]
    r = lc + l
    blk = 256 if r % 256 == 0 else 128
    kern = functools.partial(_gla_kernel, lc=lc, l=l, blk=blk)
    seq = lambda w: pl.BlockSpec((r, w), lambda b, hp: (b, hp), pipeline_mode=pl.Buffered(1))
    return pl.pallas_call(
        kern,
        grid=(n_batch, GLA_HEADS // 2),
        in_specs=[seq(2 * GLA_DK), seq(2 * GLA_DK), seq(2 * GLA_DV),
                  pl.BlockSpec((r, 128), lambda b, hp: (b, 0), pipeline_mode=pl.Buffered(1)),
                  seq(2 * GLA_DV),
                  pl.BlockSpec((2, 128, 2 * GLA_DK), lambda b, hp: (0, 0, hp)),
                  pl.BlockSpec((2, 1, 2 * GLA_DK), lambda b, hp: (0, 0, hp)),
                  pl.BlockSpec((1, GLA_DV), lambda b, hp: (0, 0))],
        out_specs=pl.BlockSpec((r, 2 * GLA_DV), lambda b, hp: (b, hp)),
        out_shape=jax.ShapeDtypeStruct((t_all, GLA_HEADS * GLA_DV), BF16),
        scratch_shapes=[pltpu.VMEM((2, r, 2 * GLA_DK), BF16), pltpu.VMEM((2, 2 * r, 2 * GLA_DK), BF16),
                        pltpu.VMEM((2, r, 2 * GLA_DK), BF16),
                        pltpu.VMEM((2, r // GLA_CHUNK, 2 * GLA_DK, 2 * GLA_DK), F32),
                        pltpu.VMEM((2 * r, 2 * GLA_DV), BF16), pltpu.VMEM((r, 2 * GLA_DV), F32)],
        compiler_params=_cparams(("arbitrary", "arbitrary")),
        name="gla_bidir",
    )(gq, gk, gv, ga, gr, wa, ba, g_o)


def _merge_kernel(h_ref, modb_ref, modc_ref, g_ref, oa_ref, op_ref, og_ref, wg_ref, wb_ref,
                  wo_ref, o_ref, *, tm, lc, tiles_per_batch):
    d = h_ref.shape[-1]
    j = pl.program_id(0) % tiles_per_batch
    shift, scale, gate = _mod_rows(modb_ref, modc_ref, j, tm, d, lc)
    h = h_ref[...]
    u = (_rms(h, g_ref[...]) * (1.0 + scale) + shift).astype(BF16)
    m = None
    for br, src in enumerate((oa_ref, op_ref, og_ref)):
        gt = _sigmoid(_dot(u, wg_ref[:, br * d:(br + 1) * d]))
        term = gt * _dot(src[...], wb_ref[br])
        m = term if m is None else m + term
    y = _dot(m.astype(BF16), wo_ref[...])
    o_ref[...] = h + gate * y


def _merge(h, mod, layer, g_mix, o_a, o_p, o_g, w_gates, w_branch, w_out, *, n_batch, lc, l):
    t_all, d = h.shape
    r = lc + l
    tm = r // 8 if (r % 128 == 0 and r // 8 >= 128) else r // 2
    tiles_per_batch = r // tm
    kern = functools.partial(_merge_kernel, tm=tm, lc=lc, tiles_per_batch=tiles_per_batch)
    tok = lambda w: pl.BlockSpec((tm, w), lambda i: (i, 0))
    return pl.pallas_call(
        kern,
        grid=(t_all // tm,),
        in_specs=[tok(d)] + _mod_specs(mod, layer, 1, tiles_per_batch, n_batch) + [
            _const_spec((1, d)), tok(BRANCH_W), tok(BRANCH_W), tok(BRANCH_W),
            _const_spec(w_gates.shape), _const_spec(w_branch.shape), _const_spec(w_out.shape)],
        out_specs=tok(d),
        out_shape=jax.ShapeDtypeStruct((t_all, d), F32),
        compiler_params=_cparams(("arbitrary",)),
        name="branch_merge",
    )(h, mod, mod, g_mix, o_a, o_p, o_g, w_gates, w_branch, w_out)


_ROPE_PARTNER = np.arange(MLA_ROPE) ^ (AXIS_DIM // 2)


def _rope_table(lc, l):
    pos = jnp.arange(l)
    row = (pos // GRID_W).astype(F32)
    col = (pos % GRID_W).astype(F32)
    inv = ROPE_THETA ** (-jnp.arange(0, AXIS_DIM, 2, dtype=F32) / AXIS_DIM)
    ang = jnp.stack([row, col], axis=-1)[:, :, None] * inv
    cos = jnp.repeat(jnp.cos(ang)[:, :, None, :], 2, axis=2).reshape(l, MLA_ROPE)
    sin = jnp.repeat(jnp.sin(ang)[:, :, None, :], 2, axis=2)
    sin = (sin * jnp.array([-1.0, 1.0], F32)[None, None, :, None]).reshape(l, MLA_ROPE)
    pad = HEAD_LANES - MLA_QK
    tab_c = jnp.concatenate([
        jnp.concatenate([jnp.ones((lc, MLA_QK), F32), jnp.zeros((lc, pad), F32)], axis=1),
        jnp.concatenate([jnp.ones((l, MLA_NOPE), F32), cos, jnp.zeros((l, pad), F32)], axis=1)],
        axis=0)
    tab_s = jnp.concatenate([
        jnp.zeros((lc, HEAD_LANES), F32),
        jnp.concatenate([jnp.zeros((l, MLA_NOPE), F32), sin, jnp.zeros((l, pad), F32)], axis=1)],
        axis=0)
    return tab_c, tab_s


def _pack_layer(p, layer):
    d = p["w_in"].shape[1]
    w_in = p["w_in"][layer]
    offs = np.cumsum([0, MLA_Q_RANK, MLA_KV_RANK, MLA_ROPE, BRANCH_W, GLA_HEADS * GLA_DK,
                      GLA_HEADS * GLA_DK, GLA_HEADS * GLA_DV, 2 * GLA_GATE_RANK,
                      GLA_HEADS * GLA_DV, N_BRANCH * d])
    cq, ckv, kr, pz, gq, gk, gv, ga, gr, gates = [w_in[:, offs[i]:offs[i + 1]] for i in range(10)]
    pad = HEAD_LANES - MLA_QK

    def rope_lanes(x):
        lead = x.shape[:-1]
        return jnp.concatenate([jnp.zeros(lead + (MLA_NOPE,), F32), x, jnp.zeros(lead + (pad,), F32)],
                               axis=-1)

    ga_ext = jnp.concatenate([ga, jnp.zeros((d, 128 - 2 * GLA_GATE_RANK), F32)], axis=1)
    w_all = jnp.concatenate([cq, ckv, rope_lanes(kr), rope_lanes(kr[:, _ROPE_PARTNER]),
                             pz, gq, gk, gv, ga_ext, gr], axis=1).astype(BF16)

    w_uq = p["w_uq"][layer].reshape(MLA_Q_RANK, MLA_HEADS, MLA_QK)
    w_uq_a = jnp.concatenate([w_uq, jnp.zeros((MLA_Q_RANK, MLA_HEADS, pad), F32)], axis=2)
    w_uq_b = rope_lanes(w_uq[:, :, MLA_NOPE + _ROPE_PARTNER])
    w_uq = jnp.concatenate([w_uq_a.reshape(MLA_Q_RANK, -1), w_uq_b.reshape(MLA_Q_RANK, -1)],
                           axis=1).astype(BF16)
    w_ukv = p["w_ukv"][layer].reshape(MLA_KV_RANK, MLA_HEADS, MLA_NOPE + MLA_V)
    w_uk = jnp.concatenate([w_ukv[:, :, :MLA_NOPE],
                            jnp.zeros((MLA_KV_RANK, MLA_HEADS, HEAD_LANES - MLA_NOPE), F32)], axis=2)
    w_uk = w_uk.reshape(MLA_KV_RANK, MLA_HEADS * HEAD_LANES).astype(BF16)
    w_uv = w_ukv[:, :, MLA_NOPE:].reshape(MLA_KV_RANK, MLA_HEADS * MLA_V).astype(BF16)

    def gain_tab(g):
        return jnp.stack([jnp.concatenate([g, jnp.zeros((pad,), F32)]),
                          rope_lanes(g[MLA_NOPE + _ROPE_PARTNER])])

    wa = jnp.zeros((2, 128, GLA_HEADS * GLA_DK), F32)
    for dr in range(2):
        wa = wa.at[dr, dr * GLA_GATE_RANK:(dr + 1) * GLA_GATE_RANK].set(p["w_a2"][layer, dr])

    return dict(
        g_mix=p["g_mix"][layer][None, :], w_all=w_all,
        g_cq=p["g_cq"][layer][None, :], w_uq=w_uq,
        g_ckv=p["g_ckv"][layer][None, :], w_uk=w_uk, w_uv=w_uv,
        gq_tab=gain_tab(p["g_qn"][layer]) * (MLA_QK ** -0.5 * math.log2(math.e)), gk_tab=gain_tab(p["g_kn"][layer]),
        w_gates=gates.astype(BF16),
        w_pool=p["w_pool"][layer].astype(BF16), pool_scale=p["pool_scale"][layer][None, :],
        wa=wa.astype(BF16), ba=p["b_a2"][layer][:, None, :], g_gla_o=p["g_gla_o"][layer][None, :],
        w_branch=p["w_branch"][layer].astype(BF16), w_out=p["w_out"][layer].astype(BF16),
    )


def kernel(x, c, ctx, c_ctx, w_ada, b_ada, g_ffn1, ffn1_w1, ffn1_w3, ffn1_w2, g_mix, w_in,
           g_cq, w_uq, g_ckv, w_ukv, g_qn, g_kn, w_pool, pool_scale, w_a2, b_a2, g_gla_o,
           w_branch, w_out, g_ffn2, ffn2_w1, ffn2_w3, ffn2_w2):
    n_batch, l, d = x.shape
    lc = ctx.shape[1]
    depth = w_ada.shape[0]
    dims = dict(n_batch=n_batch, lc=lc, l=l)
    p = dict(w_in=w_in, g_mix=g_mix, g_cq=g_cq, w_uq=w_uq, g_ckv=g_ckv, w_ukv=w_ukv, g_qn=g_qn,
             g_kn=g_kn, w_pool=w_pool, pool_scale=pool_scale, w_a2=w_a2, b_a2=b_a2,
             g_gla_o=g_gla_o, w_branch=w_branch, w_out=w_out)

    cond_rows = 16
    cond = jnp.zeros((cond_rows, d), F32).at[:n_batch].set(c).at[n_batch].set(c_ctx)
    mod = _ada_mod(cond, w_ada, b_ada).reshape(depth, cond_rows, 3, 3, d)

    tab = _rope_table(lc, l)
    h = jnp.concatenate([ctx, x], axis=1).reshape(n_batch * (lc + l), d)
    for layer in range(depth):
        last = layer == depth - 1
        pw = _pack_layer(p, layer)
        h = _ffn(h, mod, layer, 0, g_ffn1[layer], ffn1_w1[layer].astype(BF16),
                 ffn1_w3[layer].astype(BF16), ffn1_w2[layer].astype(BF16), **dims)
        q, k, v, pz, gq, gk, gv, ga, gr = _mixin(h, mod, layer, pw, tab, **dims)
        o_a = _attention(q, k, v, **dims)
        o_p = _pool(pz, pw["w_pool"], pw["pool_scale"], **dims)
        o_g = _gla(gq, gk, gv, ga, gr, pw["wa"], pw["ba"], pw["g_gla_o"], **dims)
        h = _merge(h, mod, layer, pw["g_mix"], o_a, o_p, o_g, pw["w_gates"], pw["w_branch"],
                   pw["w_out"], **dims)
        h = _ffn(h, mod, layer, 2, g_ffn2[layer], ffn2_w1[layer].astype(BF16),
                 ffn2_w3[layer].astype(BF16), ffn2_w2[layer].astype(BF16), lat_only=last, **dims)
    return h.reshape(n_batch, l, d)
```

```python
import functools
import math

import jax
import jax.numpy as jnp
import numpy as np
from jax import lax
from jax.experimental import pallas as pl
from jax.experimental.pallas import tpu as pltpu

F32 = jnp.float32
BF16 = jnp.bfloat16

EPS = 1e-6
GRID_W = 64
N_MOD = 9
MLA_HEADS = 8
MLA_Q_RANK = 256
MLA_KV_RANK = 128
MLA_NOPE = 64
MLA_ROPE = 32
MLA_V = 64
MLA_QK = MLA_NOPE + MLA_ROPE
AXIS_DIM = MLA_ROPE // 2
ROPE_THETA = 10000.0
POOL_WINDOWS = (2, 4, 8, 16)
BRANCH_W = 512
POOL_GDIM = BRANCH_W // len(POOL_WINDOWS)
GLA_HEADS = 4
GLA_DK = 64
GLA_DV = 128
GLA_GATE_RANK = 16
GLA_TAU = 16.0
GLA_CHUNK = 64
GLA_UNROLL = 4
N_BRANCH = 3
HEAD_LANES = 128

C_CQ, C_CKV, C_KRA, C_KRB, C_PZ, C_GQ, C_GK, C_GV, C_GA, C_GR, C_END = (
    0, 256, 384, 512, 640, 1152, 1408, 1664, 2176, 2304, 2816)

VMEM_LIMIT = 56 * 1024 * 1024


def _cparams(sem):
    return pltpu.CompilerParams(dimension_semantics=sem, vmem_limit_bytes=VMEM_LIMIT)


def _const_spec(shape):
    nd = len(shape)
    return pl.BlockSpec(shape, lambda *_: (0,) * nd, pipeline_mode=pl.Buffered(1))


def _dot(a, b):
    return jnp.dot(a, b, preferred_element_type=F32)


def _dot_nt(a, b):
    return lax.dot_general(a, b, (((1,), (1,)), ((), ())), preferred_element_type=F32)


def _dot_tn(a, b):
    return lax.dot_general(a, b, (((0,), (0,)), ((), ())), preferred_element_type=F32)


def _silu(x):
    return x * (1.0 / (1.0 + jnp.exp(-x)))


def _sigmoid(x):
    return 1.0 / (1.0 + jnp.exp(-x))


def _rms(x, g):
    ms = jnp.mean(x * x, axis=-1, keepdims=True)
    return x * lax.rsqrt(ms + EPS) * g


def _ada_kernel(c_ref, w_ref, b_ref, o_ref):
    s = _silu(c_ref[...]).astype(BF16)
    o_ref[...] = _dot(s, w_ref[...].astype(BF16)) + b_ref[...]


def _ada_mod(cond, w_ada, b_ada):
    depth, d, n = w_ada.shape
    rows = cond.shape[0]
    tn = 1024
    return pl.pallas_call(
        _ada_kernel,
        grid=(depth, n // tn),
        in_specs=[
            pl.BlockSpec((rows, d), lambda l, j: (0, 0)),
            pl.BlockSpec((None, d, tn), lambda l, j: (l, 0, j)),
            pl.BlockSpec((None, 1, tn), lambda l, j: (l, 0, j)),
        ],
        out_specs=pl.BlockSpec((None, rows, tn), lambda l, j: (l, 0, j)),
        out_shape=jax.ShapeDtypeStruct((depth, rows, n), F32),
        compiler_params=_cparams(("arbitrary", "arbitrary")),
        name="ada_mod",
    )(cond, w_ada, b_ada.reshape(depth, 1, n))


def _mod_rows(modb_ref, modc_ref, tile_in_batch, tm, d, lc):
    modb = modb_ref[...]
    modc = modc_ref[...]
    if lc == 0:
        return modb[0:1], modb[1:2], modb[2:3]
    row = tile_in_batch * tm + lax.broadcasted_iota(jnp.int32, (tm, d), 0)
    is_ctx = row < lc
    return tuple(jnp.where(is_ctx, modc[i:i + 1], modb[i:i + 1]) for i in range(3))


def _mod_specs(mod, layer, group, tiles_per_batch, n_batch):
    d = mod.shape[-1]
    blk = (None, None, None, 3, d)
    return [
        pl.BlockSpec(blk, lambda i: (layer, i // tiles_per_batch, group, 0, 0)),
        pl.BlockSpec(blk, lambda i: (layer, n_batch, group, 0, 0)),
    ]


def _ffn_kernel(h_ref, modb_ref, modc_ref, g_ref, w1_ref, w3_ref, w2_ref, o_ref, hid_ref,
                *, tm, lc, tiles_per_batch, col_chunk):
    d = h_ref.shape[-1]
    j = pl.program_id(0) % tiles_per_batch
    shift, scale, gate = _mod_rows(modb_ref, modc_ref, j, tm, d, lc)
    h = h_ref[...]
    u = (_rms(h, g_ref[...]) * (1.0 + scale) + shift).astype(BF16)
    d_ff = w1_ref.shape[-1]
    for c0 in range(0, d_ff, col_chunk):
        cols = slice(c0, min(c0 + col_chunk, d_ff))
        a = _dot(u, w1_ref[:, cols])
        b = _dot(u, w3_ref[:, cols])
        hid_ref[:, cols] = (_silu(a) * b).astype(BF16)
    y = _dot(hid_ref[...], w2_ref[...])
    o_ref[...] = h + 0.5 * gate * y


def _ffn(h, mod, layer, group, g, w1, w3, w2, *, n_batch, lc, l, lat_only=False):
    t_all, d = h.shape
    d_ff = w1.shape[-1]
    r = lc + l
    if lat_only:
        tm = math.gcd(lc, l) if lc else l
        tm = min(tm, 256)
        tiles_per_batch = l // tm
        skip = lc // tm
        rows_out = n_batch * l
        in_map = lambda i: ((i // tiles_per_batch) * (r // tm) + skip + i % tiles_per_batch, 0)
        lc_k = 0
    else:
        tm = r // 4
        tiles_per_batch = r // tm
        rows_out = t_all
        in_map = lambda i: (i, 0)
        lc_k = lc
    col_chunk = min(d_ff, 512)
    kern = functools.partial(_ffn_kernel, tm=tm, lc=lc_k, tiles_per_batch=tiles_per_batch,
                             col_chunk=col_chunk)
    return pl.pallas_call(
        kern,
        grid=(rows_out // tm,),
        in_specs=[pl.BlockSpec((tm, d), in_map)]
        + _mod_specs(mod, layer, group, tiles_per_batch, n_batch)
        + [_const_spec((1, d)), _const_spec((d, d_ff)), _const_spec((d, d_ff)),
           _const_spec((d_ff, d))],
        out_specs=pl.BlockSpec((tm, d), lambda i: (i, 0)),
        out_shape=jax.ShapeDtypeStruct((rows_out, d), F32),
        scratch_shapes=[pltpu.VMEM((tm, d_ff), BF16)],
        compiler_params=_cparams(("arbitrary",)),
        name="ffn_half_step",
    )(h, mod, mod, g.reshape(1, d), w1, w3, w2)


def _head_norm_rope(xa, rot, tab_a):
    ss = jnp.sum(xa * xa, axis=-1, keepdims=True) * (1.0 / MLA_QK)
    return ((xa * tab_a + rot) * lax.rsqrt(ss + EPS)).astype(BF16)


def _mixin_kernel(h_ref, modb_ref, modc_ref, g_ref, w_ref, gcq_ref, wuq_ref, gckv_ref,
                  wuk_ref, wuv_ref, gq_ref, gk_ref, tabc_ref, tabs_ref,
                  q_out, k_out, v_out, pz_out, gq_out, gk_out, gv_out, ga_out, gr_out,
                  *, tm, lc, tiles_per_batch):
    d = h_ref.shape[-1]
    j = pl.program_id(0) % tiles_per_batch
    shift, scale, _ = _mod_rows(modb_ref, modc_ref, j, tm, d, lc)
    u = (_rms(h_ref[...], g_ref[...]) * (1.0 + scale) + shift).astype(BF16)

    lowrank = _dot(u, w_ref[:, C_CQ:C_PZ])
    cqn = _rms(lowrank[:, C_CQ:C_CKV], gcq_ref[...]).astype(BF16)
    ckvn = _rms(lowrank[:, C_CKV:C_KRA], gckv_ref[...]).astype(BF16)
    kr_a = lowrank[:, C_KRA:C_KRB]
    kr_b = lowrank[:, C_KRB:C_PZ]
    q_pre = _dot(cqn, wuq_ref[...])
    k_pre = _dot(ckvn, wuk_ref[...])
    v_out[...] = _dot(ckvn, wuv_ref[...]).astype(BF16)

    tab_c = tabc_ref[...]
    tab_s = tabs_ref[...]
    gq = gq_ref[...]
    gk = gk_ref[...]
    qa_tab, qb_tab = tab_c * gq[0:1], tab_s * gq[1:2]
    ka_tab = tab_c * gk[0:1]
    k_rot = kr_b * (tab_s * gk[1:2])
    n_q = MLA_HEADS * HEAD_LANES

    def other(idx):
        if idx == 0:
            pz_out[...] = _dot(u, w_ref[:, C_PZ:C_GQ])
        elif idx == 1:
            gq_out[...] = _dot(u, w_ref[:, C_GQ:C_GK])
        elif idx == 2:
            gk_out[...] = _dot(u, w_ref[:, C_GK:C_GV])
        elif idx == 3:
            gv_out[...] = _dot(u, w_ref[:, C_GV:C_GA]).astype(BF16)
        elif idx == 4:
            ga_out[...] = _dot(u, w_ref[:, C_GA:C_GR]).astype(BF16)
        elif idx == 5:
            gr_out[...] = _dot(u, w_ref[:, C_GR:C_END])

    for hd in range(MLA_HEADS):
        cols = slice(hd * HEAD_LANES, (hd + 1) * HEAD_LANES)
        cols_b = slice(n_q + hd * HEAD_LANES, n_q + (hd + 1) * HEAD_LANES)
        pair, half = hd // 2, slice((hd % 2) * HEAD_LANES, (hd % 2 + 1) * HEAD_LANES)
        q_out[pair, :, half] = _head_norm_rope(q_pre[:, cols], q_pre[:, cols_b] * qb_tab, qa_tab)
        k_out[pair, :, half] = _head_norm_rope(k_pre[:, cols] + kr_a, k_rot, ka_tab)
        other(hd)


def _mixin(h, mod, layer, pw, tab, *, n_batch, lc, l):
    t_all, d = h.shape
    r = lc + l
    tm = r // 8 if (r % 128 == 0 and r // 8 >= 128) else r // 2
    tiles_per_batch = r // tm
    kern = functools.partial(_mixin_kernel, tm=tm, lc=lc, tiles_per_batch=tiles_per_batch)
    tok = lambda w: pl.BlockSpec((tm, w), lambda i: (i, 0))
    n_pairs = MLA_HEADS // 2
    qk_spec = pl.BlockSpec((n_pairs, tm, 2 * HEAD_LANES), lambda i: (0, i, 0))
    qk_shape = jax.ShapeDtypeStruct((n_pairs, t_all, 2 * HEAD_LANES), BF16)
    outs = [(MLA_HEADS * MLA_V, BF16), (BRANCH_W, F32), (GLA_HEADS * GLA_DK, F32),
            (GLA_HEADS * GLA_DK, F32), (GLA_HEADS * GLA_DV, BF16), (128, BF16),
            (GLA_HEADS * GLA_DV, F32)]
    return pl.pallas_call(
        kern,
        grid=(t_all // tm,),
        in_specs=[tok(d)] + _mod_specs(mod, layer, 1, tiles_per_batch, n_batch) + [
            _const_spec((1, d)), _const_spec((d, C_END)),
            _const_spec((1, MLA_Q_RANK)), _const_spec((MLA_Q_RANK, 2 * MLA_HEADS * HEAD_LANES)),
            _const_spec((1, MLA_KV_RANK)), _const_spec((MLA_KV_RANK, MLA_HEADS * HEAD_LANES)),
            _const_spec((MLA_KV_RANK, MLA_HEADS * MLA_V)),
            _const_spec((2, HEAD_LANES)), _const_spec((2, HEAD_LANES)),
            pl.BlockSpec((tm, HEAD_LANES), lambda i: (i % tiles_per_batch, 0)),
            pl.BlockSpec((tm, HEAD_LANES), lambda i: (i % tiles_per_batch, 0)),
        ],
        out_specs=[qk_spec, qk_spec] + [tok(w) for w, _ in outs],
        out_shape=[qk_shape, qk_shape] + [jax.ShapeDtypeStruct((t_all, w), dt) for w, dt in outs],
        compiler_params=_cparams(("arbitrary",)),
        name="mixer_in_proj",
    )(h, mod, mod, pw["g_mix"], pw["w_all"], pw["g_cq"], pw["w_uq"], pw["g_ckv"],
      pw["w_uk"], pw["w_uv"], pw["gq_tab"], pw["gk_tab"], tab[0], tab[1])


ONES_ROWS = 16
KEY_CHUNK = 256
SCORE_LOOKAHEAD = 5
BOUND_MARGIN = 1.02
MAX_SAFE_BOUND = 50.0
NORM_ROWS = 1088


def _pair_queries(q):
    lane = lax.broadcasted_iota(jnp.int32, q.shape, 1)
    zero = jnp.zeros_like(q)
    return jnp.concatenate([jnp.where(lane < HEAD_LANES, q, zero),
                            jnp.where(lane < HEAD_LANES, zero, q)], axis=0)


def _score_bound(qbd, kmax_row):
    qn2 = _dot_nt(jnp.ones((8, qbd.shape[1]), BF16), qbd * qbd)[0:1]
    return jnp.sqrt(qn2 * kmax_row) * BOUND_MARGIN


def _attn_tile(q_ref, k_ref, vt_ref, o_ref, hp, n_keys, bound=None):
    tq = q_ref.shape[1]
    vrows = slice(hp * 2 * MLA_V, (hp + 1) * 2 * MLA_V)
    qbd = _pair_queries(q_ref[hp])
    chunks = [(k0, min(KEY_CHUNK, n_keys - k0)) for k0 in range(0, n_keys, KEY_CHUNK)]

    def scores(k0, kn):
        return _dot_nt(k_ref[hp, k0:k0 + kn, :], qbd)

    pending = [scores(*ch) for ch in chunks[:SCORE_LOOKAHEAD]]
    m = acc = None
    for ci, (k0, kn) in enumerate(chunks):
        if ci + SCORE_LOOKAHEAD < len(chunks):
            pending.append(scores(*chunks[ci + SCORE_LOOKAHEAD]))
        st = pending.pop(0)
        vt = jnp.concatenate([vt_ref[vrows, k0:k0 + kn], jnp.ones((ONES_ROWS, kn), BF16)], axis=0)
        if bound is not None:
            part = _dot(vt, jnp.exp2(st - bound).astype(BF16))
            acc = part if acc is None else acc + part
            continue
        cm = jnp.max(st, axis=0, keepdims=True)
        if m is None:
            m = cm
            acc = _dot(vt, jnp.exp2(st - m).astype(BF16))
        else:
            m_new = jnp.maximum(m, cm)
            acc = acc * jnp.exp2(m - m_new) + _dot(vt, jnp.exp2(st - m_new).astype(BF16))
            m = m_new
    inv = 1.0 / acc[2 * MLA_V:2 * MLA_V + 1]
    o_t = jnp.concatenate([acc[0:MLA_V, 0:tq] * inv[:, 0:tq],
                           acc[MLA_V:2 * MLA_V, tq:] * inv[:, tq:]], axis=0)
    o_ref[:, vrows] = o_t.T.astype(o_ref.dtype)


def _attn_kernel(q_ref, k_ref, v_ref, o_ref, vt_ref, kmax_ref, *, lc):
    r = k_ref.shape[1]
    tq = q_ref.shape[1]
    i = pl.program_id(1)
    pairs = range(MLA_HEADS // 2)

    @pl.when(i == 0)
    def _():
        vt_ref[...] = v_ref[...].T
        sel = jnp.where((lax.broadcasted_iota(jnp.int32, (2 * HEAD_LANES, 2 * tq), 0) < HEAD_LANES)
                        == (lax.broadcasted_iota(jnp.int32, (2 * HEAD_LANES, 2 * tq), 1) < tq),
                        1.0, 0.0).astype(BF16)
        for hp in pairs:
            kmax = None
            for r0 in range(0, r, NORM_ROWS):
                k = k_ref[hp, r0:min(r0 + NORM_ROWS, r), :]
                cur = jnp.max(_dot(k * k, sel), axis=0, keepdims=True)
                kmax = cur if kmax is None else jnp.maximum(kmax, cur)
            kmax_ref[hp] = jnp.broadcast_to(kmax, kmax_ref.shape[1:])

    if lc:
        @pl.when(i == 0)
        def _():
            for hp in pairs:
                _attn_tile(q_ref, k_ref, vt_ref, o_ref, hp, lc)

    bounds = [_score_bound(_pair_queries(q_ref[hp]), kmax_ref[hp][0:1]) for hp in pairs]
    worst = bounds[0]
    for bnd in bounds[1:]:
        worst = jnp.maximum(worst, bnd)
    safe = jnp.max(worst) <= MAX_SAFE_BOUND
    latent = (i > 0) if lc else (i >= 0)

    @pl.when(jnp.logical_and(latent, safe))
    def _():
        for hp in pairs:
            _attn_tile(q_ref, k_ref, vt_ref, o_ref, hp, r, bounds[hp])

    @pl.when(jnp.logical_and(latent, jnp.logical_not(safe)))
    def _():
        for hp in pairs:
            _attn_tile(q_ref, k_ref, vt_ref, o_ref, hp, r)


def _attention(q, k, v, *, n_batch, lc, l):
    n_pairs, t_all, _ = q.shape
    r = lc + l
    tq = lc
    tiles = r // tq
    kern = functools.partial(_attn_kernel, lc=lc)
    return pl.pallas_call(
        kern,
        grid=(n_batch, tiles),
        in_specs=[
            pl.BlockSpec((n_pairs, tq, 2 * HEAD_LANES), lambda b, i: (0, b * tiles + i, 0)),
            pl.BlockSpec((n_pairs, r, 2 * HEAD_LANES), lambda b, i: (0, b, 0)),
            pl.BlockSpec((r, MLA_HEADS * MLA_V), lambda b, i: (b, 0)),
        ],
        out_specs=pl.BlockSpec((tq, MLA_HEADS * MLA_V), lambda b, i: (b * tiles + i, 0)),
        out_shape=jax.ShapeDtypeStruct((t_all, MLA_HEADS * MLA_V), BF16),
        scratch_shapes=[pltpu.VMEM((MLA_HEADS * MLA_V, r), BF16),
                        pltpu.VMEM((n_pairs, 8, 2 * tq), F32)],
        compiler_params=_cparams(("arbitrary", "arbitrary")),
        name="mla_attention",
    )(q, k, v)


def _pool_kernel(p_ref, w_ref, s_ref, o_ref, *, lc, l):
    r = lc + l
    t = lax.broadcasted_iota(jnp.int32, (r, POOL_GDIM), 0)
    in_lat = t >= lc
    pos = jnp.where(in_lat, t - lc, t)
    seg_len = jnp.where(in_lat, l, lc)

    def shifted(x, dlt):
        src = pos + dlt
        rolled = pltpu.roll(x, (-dlt) % r, axis=0)
        return jnp.where(src >= 0, jnp.where(src < seg_len, rolled, 0.0), 0.0)

    for g, w in enumerate(POOL_WINDOWS):
        half = w // 2
        cols = slice(g * POOL_GDIM, (g + 1) * POOL_GDIM)
        x = p_ref[:, cols]
        right = x
        left = x
        step = 1
        while step < half:
            right = right + shifted(right, step)
            left = left + shifted(left, -step)
            step *= 2
        win = shifted(left, -1) + right
        cnt = (jnp.minimum(pos + half, seg_len) - jnp.maximum(pos - half, 0)).astype(F32)
        pooled = (win / cnt - x).astype(BF16)
        o_ref[:, cols] = (_dot(pooled, w_ref[g]) * s_ref[:, cols]).astype(o_ref.dtype)


def _pool(pz, w_pool, pool_scale, *, n_batch, lc, l):
    t_all = pz.shape[0]
    r = lc + l
    kern = functools.partial(_pool_kernel, lc=lc, l=l)
    return pl.pallas_call(
        kern,
        grid=(n_batch,),
        in_specs=[pl.BlockSpec((r, BRANCH_W), lambda b: (b, 0)),
                  _const_spec(w_pool.shape), _const_spec((1, BRANCH_W))],
        out_specs=pl.BlockSpec((r, BRANCH_W), lambda b: (b, 0)),
        out_shape=jax.ShapeDtypeStruct((t_all, BRANCH_W), BF16),
        compiler_params=_cparams(("arbitrary",)),
        name="multiscale_pool",
    )(pz, w_pool, pool_scale)


def _gla_kernel(q_ref, k_ref, v_ref, ga_ref, gr_ref, wa_ref, ba_ref, go_ref, o_ref,
                qt_s, kts_s, ke_s, dec_s, vbd_s, og_s, *, lc, l, blk):
    r = lc + l
    c = GLA_CHUNK
    n_chunks = r // c
    n_ctx = lc // c
    cpb = blk // c
    w = 2 * GLA_DK
    bi_ = lax.broadcasted_iota(jnp.int32, (blk, blk), 0)
    bj_ = lax.broadcasted_iota(jnp.int32, (blk, blk), 1)
    same_chunk = (bi_ // c) == (bj_ // c)
    ones_m = jnp.where(same_chunk, 1.0, 0.0).astype(BF16)
    ind = jnp.where(lax.broadcasted_iota(jnp.int32, (blk, cpb * w), 0) // c
                    == lax.broadcasted_iota(jnp.int32, (blk, cpb * w), 1) // w, 1.0, 0.0).astype(BF16)
    head0_k = lax.broadcasted_iota(jnp.int32, (blk, w), 1) < GLA_DK
    head0_v = lax.broadcasted_iota(jnp.int32, (blk, 2 * GLA_DV), 1) < GLA_DV
    ci_ = lax.broadcasted_iota(jnp.int32, (c, w), 0)
    cj_ = lax.broadcasted_iota(jnp.int32, (c, w), 1) % c
    state_diag = ((lax.broadcasted_iota(jnp.int32, (w, 2 * GLA_DV), 0) < GLA_DK)
                  == (lax.broadcasted_iota(jnp.int32, (w, 2 * GLA_DV), 1) < GLA_DV))

    def split_store(dst, b, x0, x1):
        for cc in range(cpb):
            base = (b * cpb + cc) * 2 * c
            dst[pl.ds(pl.multiple_of(base, c), c), :] = x0[cc * c:(cc + 1) * c]
            dst[pl.ds(pl.multiple_of(base + c, c), c), :] = x1[cc * c:(cc + 1) * c]

    def vprep(b, carry):
        v = v_ref[pl.ds(pl.multiple_of(b * blk, blk), blk), :]
        zero = jnp.zeros_like(v)
        split_store(vbd_s, b, jnp.where(head0_v, v, zero), jnp.where(head0_v, zero, v))
        return carry

    lax.fori_loop(0, r // blk, vprep, 0)
    og_s[...] = jnp.zeros_like(og_s)

    dirs = (0, 1)
    tri_m = [jnp.where(same_chunk, jnp.where((bj_ <= bi_) if d == 0 else (bj_ >= bi_), 1.0, 0.0),
                       0.0).astype(BF16) for d in dirs]
    causal = [(cj_ <= ci_) if d == 0 else (cj_ >= ci_) for d in dirs]

    def prep(b, carry):
        rows = pl.ds(pl.multiple_of(b * blk, blk), blk)
        ga = ga_ref[rows, :]
        logit = [_dot(ga, wa_ref[d]) + ba_ref[d] for d in dirs]
        la = [(jnp.minimum(x, 0.0) - jnp.log(1.0 + jnp.exp(-jnp.abs(x)))) * (1.0 / GLA_TAU)
              for x in logit]
        hi = [x.astype(BF16) for x in la]
        hl = [jnp.concatenate([hi[d], (la[d] - hi[d].astype(F32)).astype(BF16)], axis=1)
              for d in dirs]
        cs = [_dot(tri_m[d], hl[d]) for d in dirs]
        ts = [_dot(ones_m, hl[d]) for d in dirs]
        totb = [_dot_tn(hl[d], ind) for d in dirs]
        q = q_ref[rows, :] * (GLA_DK ** -0.5)
        k = k_ref[rows, :]
        for d in dirs:
            bcum = cs[d][:, :w] + cs[d][:, w:]
            tot = ts[d][:, :w] + ts[d][:, w:]
            for cc in range(cpb):
                dec_s[d, b * cpb + cc] = jnp.exp(totb[d][:w, cc * w:(cc + 1) * w]
                                                 + totb[d][w:, cc * w:(cc + 1) * w])
            qt_s[d, rows, :] = (q * jnp.exp(bcum)).astype(BF16)
            ke_s[d, rows, :] = (k * jnp.exp(tot - bcum)).astype(BF16)
            kt = (k * jnp.exp(-bcum)).astype(BF16)
            zero = jnp.zeros_like(kt)
            split_store(kts_s.at[d], b, jnp.where(head0_k, kt, zero), jnp.where(head0_k, zero, kt))
        return carry

    lax.fori_loop(0, r // blk, prep, 0)

    def scan(t, sts):
        work = []
        for u in range(GLA_UNROLL):
            i = t * GLA_UNROLL + u
            for d in dirs:
                if d == 0:
                    n = i
                else:
                    n = jnp.where(i < n_ctx, n_ctx - 1 - i, n_chunks - 1 - (i - n_ctx))
                rows = pl.ds(pl.multiple_of(n * c, c), c)
                rows2 = pl.ds(pl.multiple_of(n * 2 * c, 2 * c), 2 * c)
                qc = qt_s[d, rows, :]
                a = jnp.where(causal[d], _dot_nt(qc, kts_s[d, rows2, :]), 0.0).astype(BF16)
                dst = jnp.where(state_diag, _dot_tn(ke_s[d, rows, :], v_ref[rows, :]), 0.0)
                work.append((d, n, rows, rows2, qc, a, dst))
        sts = list(sts)
        for d, n, rows, rows2, qc, a, dst in work:
            o = _dot(jnp.concatenate([a, qc], axis=1),
                     jnp.concatenate([vbd_s[rows2, :], sts[d].astype(BF16)], axis=0))
            og_s[rows, :] = og_s[rows, :] + o
            dec = dec_s[d, n]
            sts[d] = sts[d] * jnp.concatenate([dec, dec], axis=1) + dst
        return tuple(sts)

    zero_state = jnp.zeros((w, 2 * GLA_DV), F32)
    lax.fori_loop(0, n_chunks // GLA_UNROLL, scan, (zero_state, zero_state))

    for hh in range(2):
        vc = slice(hh * GLA_DV, (hh + 1) * GLA_DV)
        on = _rms(og_s[:, vc], go_ref[...])
        o_ref[:, vc] = (on * _silu(gr_ref[:, vc])).astype(o_ref.dtype)


def _gla(gq, gk, gv, ga, gr, wa, ba, g_o, *, n_batch, lc, l):
    t_all = gq.shape[0]
    r = lc + l
    blk = 256 if r % 256 == 0 else 128
    kern = functools.partial(_gla_kernel, lc=lc, l=l, blk=blk)
    seq = lambda w: pl.BlockSpec((r, w), lambda b, hp: (b, hp), pipeline_mode=pl.Buffered(1))
    return pl.pallas_call(
        kern,
        grid=(n_batch, GLA_HEADS // 2),
        in_specs=[seq(2 * GLA_DK), seq(2 * GLA_DK), seq(2 * GLA_DV),
                  pl.BlockSpec((r, 128), lambda b, hp: (b, 0), pipeline_mode=pl.Buffered(1)),
                  seq(2 * GLA_DV),
                  pl.BlockSpec((2, 128, 2 * GLA_DK), lambda b, hp: (0, 0, hp)),
                  pl.BlockSpec((2, 1, 2 * GLA_DK), lambda b, hp: (0, 0, hp)),
                  pl.BlockSpec((1, GLA_DV), lambda b, hp: (0, 0))],
        out_specs=pl.BlockSpec((r, 2 * GLA_DV), lambda b, hp: (b, hp)),
        out_shape=jax.ShapeDtypeStruct((t_all, GLA_HEADS * GLA_DV), BF16),
        scratch_shapes=[pltpu.VMEM((2, r, 2 * GLA_DK), BF16), pltpu.VMEM((2, 2 * r, 2 * GLA_DK), BF16),
                        pltpu.VMEM((2, r, 2 * GLA_DK), BF16),
                        pltpu.VMEM((2, r // GLA_CHUNK, 2 * GLA_DK, 2 * GLA_DK), F32),
                        pltpu.VMEM((2 * r, 2 * GLA_DV), BF16), pltpu.VMEM((r, 2 * GLA_DV), F32)],
        compiler_params=_cparams(("arbitrary", "arbitrary")),
        name="gla_bidir",
    )(gq, gk, gv, ga, gr, wa, ba, g_o)


def _merge_kernel(h_ref, modb_ref, modc_ref, g_ref, oa_ref, op_ref, og_ref, wg_ref, wb_ref,
                  wo_ref, o_ref, *, tm, lc, tiles_per_batch):
    d = h_ref.shape[-1]
    j = pl.program_id(0) % tiles_per_batch
    shift, scale, gate = _mod_rows(modb_ref, modc_ref, j, tm, d, lc)
    h = h_ref[...]
    u = (_rms(h, g_ref[...]) * (1.0 + scale) + shift).astype(BF16)
    m = None
    for br, src in enumerate((oa_ref, op_ref, og_ref)):
        gt = _sigmoid(_dot(u, wg_ref[:, br * d:(br + 1) * d]))
        term = gt * _dot(src[...], wb_ref[br])
        m = term if m is None else m + term
    y = _dot(m.astype(BF16), wo_ref[...])
    o_ref[...] = h + gate * y


def _merge(h, mod, layer, g_mix, o_a, o_p, o_g, w_gates, w_branch, w_out, *, n_batch, lc, l):
    t_all, d = h.shape
    r = lc + l
    tm = r // 8 if (r % 128 == 0 and r // 8 >= 128) else r // 2
    tiles_per_batch = r // tm
    kern = functools.partial(_merge_kernel, tm=tm, lc=lc, tiles_per_batch=tiles_per_batch)
    tok = lambda w: pl.BlockSpec((tm, w), lambda i: (i, 0))
    return pl.pallas_call(
        kern,
        grid=(t_all // tm,),
        in_specs=[tok(d)] + _mod_specs(mod, layer, 1, tiles_per_batch, n_batch) + [
            _const_spec((1, d)), tok(BRANCH_W), tok(BRANCH_W), tok(BRANCH_W),
            _const_spec(w_gates.shape), _const_spec(w_branch.shape), _const_spec(w_out.shape)],
        out_specs=tok(d),
        out_shape=jax.ShapeDtypeStruct((t_all, d), F32),
        compiler_params=_cparams(("arbitrary",)),
        name="branch_merge",
    )(h, mod, mod, g_mix, o_a, o_p, o_g, w_gates, w_branch, w_out)


_ROPE_PARTNER = np.arange(MLA_ROPE) ^ (AXIS_DIM // 2)


def _rope_table(lc, l):
    pos = jnp.arange(l)
    row = (pos // GRID_W).astype(F32)
    col = (pos % GRID_W).astype(F32)
    inv = ROPE_THETA ** (-jnp.arange(0, AXIS_DIM, 2, dtype=F32) / AXIS_DIM)
    ang = jnp.stack([row, col], axis=-1)[:, :, None] * inv
    cos = jnp.repeat(jnp.cos(ang)[:, :, None, :], 2, axis=2).reshape(l, MLA_ROPE)
    sin = jnp.repeat(jnp.sin(ang)[:, :, None, :], 2, axis=2)
    sin = (sin * jnp.array([-1.0, 1.0], F32)[None, None, :, None]).reshape(l, MLA_ROPE)
    pad = HEAD_LANES - MLA_QK
    tab_c = jnp.concatenate([
        jnp.concatenate([jnp.ones((lc, MLA_QK), F32), jnp.zeros((lc, pad), F32)], axis=1),
        jnp.concatenate([jnp.ones((l, MLA_NOPE), F32), cos, jnp.zeros((l, pad), F32)], axis=1)],
        axis=0)
    tab_s = jnp.concatenate([
        jnp.zeros((lc, HEAD_LANES), F32),
        jnp.concatenate([jnp.zeros((l, MLA_NOPE), F32), sin, jnp.zeros((l, pad), F32)], axis=1)],
        axis=0)
    return tab_c, tab_s


def _pack_layer(p, layer):
    d = p["w_in"].shape[1]
    w_in = p["w_in"][layer]
    offs = np.cumsum([0, MLA_Q_RANK, MLA_KV_RANK, MLA_ROPE, BRANCH_W, GLA_HEADS * GLA_DK,
                      GLA_HEADS * GLA_DK, GLA_HEADS * GLA_DV, 2 * GLA_GATE_RANK,
                      GLA_HEADS * GLA_DV, N_BRANCH * d])
    cq, ckv, kr, pz, gq, gk, gv, ga, gr, gates = [w_in[:, offs[i]:offs[i + 1]] for i in range(10)]
    pad = HEAD_LANES - MLA_QK

    def rope_lanes(x):
        lead = x.shape[:-1]
        return jnp.concatenate([jnp.zeros(lead + (MLA_NOPE,), F32), x, jnp.zeros(lead + (pad,), F32)],
                               axis=-1)

    ga_ext = jnp.concatenate([ga, jnp.zeros((d, 128 - 2 * GLA_GATE_RANK), F32)], axis=1)
    w_all = jnp.concatenate([cq, ckv, rope_lanes(kr), rope_lanes(kr[:, _ROPE_PARTNER]),
                             pz, gq, gk, gv, ga_ext, gr], axis=1).astype(BF16)

    w_uq = p["w_uq"][layer].reshape(MLA_Q_RANK, MLA_HEADS, MLA_QK)
    w_uq_a = jnp.concatenate([w_uq, jnp.zeros((MLA_Q_RANK, MLA_HEADS, pad), F32)], axis=2)
    w_uq_b = rope_lanes(w_uq[:, :, MLA_NOPE + _ROPE_PARTNER])
    w_uq = jnp.concatenate([w_uq_a.reshape(MLA_Q_RANK, -1), w_uq_b.reshape(MLA_Q_RANK, -1)],
                           axis=1).astype(BF16)
    w_ukv = p["w_ukv"][layer].reshape(MLA_KV_RANK, MLA_HEADS, MLA_NOPE + MLA_V)
    w_uk = jnp.concatenate([w_ukv[:, :, :MLA_NOPE],
                            jnp.zeros((MLA_KV_RANK, MLA_HEADS, HEAD_LANES - MLA_NOPE), F32)], axis=2)
    w_uk = w_uk.reshape(MLA_KV_RANK, MLA_HEADS * HEAD_LANES).astype(BF16)
    w_uv = w_ukv[:, :, MLA_NOPE:].reshape(MLA_KV_RANK, MLA_HEADS * MLA_V).astype(BF16)

    def gain_tab(g):
        return jnp.stack([jnp.concatenate([g, jnp.zeros((pad,), F32)]),
                          rope_lanes(g[MLA_NOPE + _ROPE_PARTNER])])

    wa = jnp.zeros((2, 128, GLA_HEADS * GLA_DK), F32)
    for dr in range(2):
        wa = wa.at[dr, dr * GLA_GATE_RANK:(dr + 1) * GLA_GATE_RANK].set(p["w_a2"][layer, dr])

    return dict(
        g_mix=p["g_mix"][layer][None, :], w_all=w_all,
        g_cq=p["g_cq"][layer][None, :], w_uq=w_uq,
        g_ckv=p["g_ckv"][layer][None, :], w_uk=w_uk, w_uv=w_uv,
        gq_tab=gain_tab(p["g_qn"][layer]) * (MLA_QK ** -0.5 * math.log2(math.e)), gk_tab=gain_tab(p["g_kn"][layer]),
        w_gates=gates.astype(BF16),
        w_pool=p["w_pool"][layer].astype(BF16), pool_scale=p["pool_scale"][layer][None, :],
        wa=wa.astype(BF16), ba=p["b_a2"][layer][:, None, :], g_gla_o=p["g_gla_o"][layer][None, :],
        w_branch=p["w_branch"][layer].astype(BF16), w_out=p["w_out"][layer].astype(BF16),
    )


def kernel(x, c, ctx, c_ctx, w_ada, b_ada, g_ffn1, ffn1_w1, ffn1_w3, ffn1_w2, g_mix, w_in,
           g_cq, w_uq, g_ckv, w_ukv, g_qn, g_kn, w_pool, pool_scale, w_a2, b_a2, g_gla_o,
           w_branch, w_out, g_ffn2, ffn2_w1, ffn2_w3, ffn2_w2):
    n_batch, l, d = x.shape
    lc = ctx.shape[1]
    depth = w_ada.shape[0]
    dims = dict(n_batch=n_batch, lc=lc, l=l)
    p = dict(w_in=w_in, g_mix=g_mix, g_cq=g_cq, w_uq=w_uq, g_ckv=g_ckv, w_ukv=w_ukv, g_qn=g_qn,
             g_kn=g_kn, w_pool=w_pool, pool_scale=pool_scale, w_a2=w_a2, b_a2=b_a2,
             g_gla_o=g_gla_o, w_branch=w_branch, w_out=w_out)

    cond_rows = 16
    cond = jnp.zeros((cond_rows, d), F32).at[:n_batch].set(c).at[n_batch].set(c_ctx)
    mod = _ada_mod(cond, w_ada, b_ada).reshape(depth, cond_rows, 3, 3, d)

    tab = _rope_table(lc, l)
    h = jnp.concatenate([ctx, x], axis=1).reshape(n_batch * (lc + l), d)
    for layer in range(depth):
        last = layer == depth - 1
        pw = _pack_layer(p, layer)
        h = _ffn(h, mod, layer, 0, g_ffn1[layer], ffn1_w1[layer].astype(BF16),
                 ffn1_w3[layer].astype(BF16), ffn1_w2[layer].astype(BF16), **dims)
        q, k, v, pz, gq, gk, gv, ga, gr = _mixin(h, mod, layer, pw, tab, **dims)
        o_a = _attention(q, k, v, **dims)
        o_p = _pool(pz, pw["w_pool"], pw["pool_scale"], **dims)
        o_g = _gla(gq, gk, gv, ga, gr, pw["wa"], pw["ba"], pw["g_gla_o"], **dims)
        h = _merge(h, mod, layer, pw["g_mix"], o_a, o_p, o_g, pw["w_gates"], pw["w_branch"],
                   pw["w_out"], **dims)
        h = _ffn(h, mod, layer, 2, g_ffn2[layer], ffn2_w1[layer].astype(BF16),
                 ffn2_w3[layer].astype(BF16), ffn2_w2[layer].astype(BF16), lat_only=last, **dims)
    return h.reshape(n_batch, l, d)
```

```python
import functools
import math

import jax
import jax.numpy as jnp
import numpy as np
from jax import lax
from jax.experimental import pallas as pl
from jax.experimental.pallas import tpu as pltpu

F32 = jnp.float32
BF16 = jnp.bfloat16

EPS = 1e-6
GRID_W = 64
N_MOD = 9
MLA_HEADS = 8
MLA_Q_RANK = 256
MLA_KV_RANK = 128
MLA_NOPE = 64
MLA_ROPE = 32
MLA_V = 64
MLA_QK = MLA_NOPE + MLA_ROPE
AXIS_DIM = MLA_ROPE // 2
ROPE_THETA = 10000.0
POOL_WINDOWS = (2, 4, 8, 16)
BRANCH_W = 512
POOL_GDIM = BRANCH_W // len(POOL_WINDOWS)
GLA_HEADS = 4
GLA_DK = 64
GLA_DV = 128
GLA_GATE_RANK = 16
GLA_TAU = 16.0
GLA_CHUNK = 64
GLA_UNROLL = 4
N_BRANCH = 3
HEAD_LANES = 128

C_CQ, C_CKV, C_KRA, C_KRB, C_PZ, C_GQ, C_GK, C_GV, C_GA, C_GR, C_END = (
    0, 256, 384, 512, 640, 1152, 1408, 1664, 2176, 2304, 2816)

VMEM_LIMIT = 56 * 1024 * 1024


def _cparams(sem):
    return pltpu.CompilerParams(dimension_semantics=sem, vmem_limit_bytes=VMEM_LIMIT)


def _const_spec(shape):
    nd = len(shape)
    return pl.BlockSpec(shape, lambda *_: (0,) * nd, pipeline_mode=pl.Buffered(1))


def _dot(a, b):
    return jnp.dot(a, b, preferred_element_type=F32)


def _dot_nt(a, b):
    return lax.dot_general(a, b, (((1,), (1,)), ((), ())), preferred_element_type=F32)


def _dot_tn(a, b):
    return lax.dot_general(a, b, (((0,), (0,)), ((), ())), preferred_element_type=F32)


def _silu(x):
    return x * (1.0 / (1.0 + jnp.exp(-x)))


def _sigmoid(x):
    return 1.0 / (1.0 + jnp.exp(-x))


def _rms(x, g):
    ms = jnp.mean(x * x, axis=-1, keepdims=True)
    return x * lax.rsqrt(ms + EPS) * g


def _ada_kernel(c_ref, w_ref, b_ref, o_ref):
    s = _silu(c_ref[...]).astype(BF16)
    o_ref[...] = _dot(s, w_ref[...].astype(BF16)) + b_ref[...]


def _ada_mod(cond, w_ada, b_ada):
    depth, d, n = w_ada.shape
    rows = cond.shape[0]
    tn = 1024
    return pl.pallas_call(
        _ada_kernel,
        grid=(depth, n // tn),
        in_specs=[
            pl.BlockSpec((rows, d), lambda l, j: (0, 0)),
            pl.BlockSpec((None, d, tn), lambda l, j: (l, 0, j)),
            pl.BlockSpec((None, 1, tn), lambda l, j: (l, 0, j)),
        ],
        out_specs=pl.BlockSpec((None, rows, tn), lambda l, j: (l, 0, j)),
        out_shape=jax.ShapeDtypeStruct((depth, rows, n), F32),
        compiler_params=_cparams(("arbitrary", "arbitrary")),
        name="ada_mod",
    )(cond, w_ada, b_ada.reshape(depth, 1, n))


def _mod_rows(modb_ref, modc_ref, tile_in_batch, tm, d, lc):
    modb = modb_ref[...]
    modc = modc_ref[...]
    if lc == 0:
        return modb[0:1], modb[1:2], modb[2:3]
    row = tile_in_batch * tm + lax.broadcasted_iota(jnp.int32, (tm, d), 0)
    is_ctx = row < lc
    return tuple(jnp.where(is_ctx, modc[i:i + 1], modb[i:i + 1]) for i in range(3))


def _mod_specs(mod, layer, group, tiles_per_batch, n_batch):
    d = mod.shape[-1]
    blk = (None, None, None, 3, d)
    return [
        pl.BlockSpec(blk, lambda i: (layer, i // tiles_per_batch, group, 0, 0)),
        pl.BlockSpec(blk, lambda i: (layer, n_batch, group, 0, 0)),
    ]


def _ffn_kernel(h_ref, modb_ref, modc_ref, g_ref, w1_ref, w3_ref, w2_ref, o_ref, hid_ref,
                *, tm, lc, tiles_per_batch, col_chunk):
    d = h_ref.shape[-1]
    j = pl.program_id(0) % tiles_per_batch
    shift, scale, gate = _mod_rows(modb_ref, modc_ref, j, tm, d, lc)
    h = h_ref[...]
    u = (_rms(h, g_ref[...]) * (1.0 + scale) + shift).astype(BF16)
    d_ff = w1_ref.shape[-1]
    for c0 in range(0, d_ff, col_chunk):
        cols = slice(c0, min(c0 + col_chunk, d_ff))
        a = _dot(u, w1_ref[:, cols])
        b = _dot(u, w3_ref[:, cols])
        hid_ref[:, cols] = (_silu(a) * b).astype(BF16)
    y = _dot(hid_ref[...], w2_ref[...])
    o_ref[...] = h + 0.5 * gate * y


def _ffn(h, mod, layer, group, g, w1, w3, w2, *, n_batch, lc, l, lat_only=False):
    t_all, d = h.shape
    d_ff = w1.shape[-1]
    r = lc + l
    if lat_only:
        tm = math.gcd(lc, l) if lc else l
        tm = min(tm, 256)
        tiles_per_batch = l // tm
        skip = lc // tm
        rows_out = n_batch * l
        in_map = lambda i: ((i // tiles_per_batch) * (r // tm) + skip + i % tiles_per_batch, 0)
        lc_k = 0
    else:
        tm = r // 4
        tiles_per_batch = r // tm
        rows_out = t_all
        in_map = lambda i: (i, 0)
        lc_k = lc
    col_chunk = min(d_ff, 512)
    kern = functools.partial(_ffn_kernel, tm=tm, lc=lc_k, tiles_per_batch=tiles_per_batch,
                             col_chunk=col_chunk)
    return pl.pallas_call(
        kern,
        grid=(rows_out // tm,),
        in_specs=[pl.BlockSpec((tm, d), in_map)]
        + _mod_specs(mod, layer, group, tiles_per_batch, n_batch)
        + [_const_spec((1, d)), _const_spec((d, d_ff)), _const_spec((d, d_ff)),
           _const_spec((d_ff, d))],
        out_specs=pl.BlockSpec((tm, d), lambda i: (i, 0)),
        out_shape=jax.ShapeDtypeStruct((rows_out, d), F32),
        scratch_shapes=[pltpu.VMEM((tm, d_ff), BF16)],
        compiler_params=_cparams(("arbitrary",)),
        name="ffn_half_step",
    )(h, mod, mod, g.reshape(1, d), w1, w3, w2)


def _head_norm_rope(xa, rot, tab_a):
    ss = jnp.sum(xa * xa, axis=-1, keepdims=True) * (1.0 / MLA_QK)
    return ((xa * tab_a + rot) * lax.rsqrt(ss + EPS)).astype(BF16)


def _mixin_kernel(h_ref, modb_ref, modc_ref, g_ref, w_ref, gcq_ref, wuq_ref, gckv_ref,
                  wuk_ref, wuv_ref, gq_ref, gk_ref, tabc_ref, tabs_ref,
                  q_out, k_out, v_out, pz_out, gq_out, gk_out, gv_out, ga_out, gr_out,
                  *, tm, lc, tiles_per_batch):
    d = h_ref.shape[-1]
    j = pl.program_id(0) % tiles_per_batch
    shift, scale, _ = _mod_rows(modb_ref, modc_ref, j, tm, d, lc)
    u = (_rms(h_ref[...], g_ref[...]) * (1.0 + scale) + shift).astype(BF16)

    lowrank = _dot(u, w_ref[:, C_CQ:C_PZ])
    cqn = _rms(lowrank[:, C_CQ:C_CKV], gcq_ref[...]).astype(BF16)
    ckvn = _rms(lowrank[:, C_CKV:C_KRA], gckv_ref[...]).astype(BF16)
    kr_a = lowrank[:, C_KRA:C_KRB]
    kr_b = lowrank[:, C_KRB:C_PZ]
    q_pre = _dot(cqn, wuq_ref[...])
    k_pre = _dot(ckvn, wuk_ref[...])
    v_out[...] = _dot(ckvn, wuv_ref[...]).astype(BF16)

    tab_c = tabc_ref[...]
    tab_s = tabs_ref[...]
    gq = gq_ref[...]
    gk = gk_ref[...]
    qa_tab, qb_tab = tab_c * gq[0:1], tab_s * gq[1:2]
    ka_tab = tab_c * gk[0:1]
    k_rot = kr_b * (tab_s * gk[1:2])
    n_q = MLA_HEADS * HEAD_LANES

    def other(idx):
        if idx == 0:
            pz_out[...] = _dot(u, w_ref[:, C_PZ:C_GQ])
        elif idx == 1:
            gq_out[...] = _dot(u, w_ref[:, C_GQ:C_GK])
        elif idx == 2:
            gk_out[...] = _dot(u, w_ref[:, C_GK:C_GV])
        elif idx == 3:
            gv_out[...] = _dot(u, w_ref[:, C_GV:C_GA]).astype(BF16)
        elif idx == 4:
            ga_out[...] = _dot(u, w_ref[:, C_GA:C_GR]).astype(BF16)
        elif idx == 5:
            gr_out[...] = _dot(u, w_ref[:, C_GR:C_END])

    for hd in range(MLA_HEADS):
        cols = slice(hd * HEAD_LANES, (hd + 1) * HEAD_LANES)
        cols_b = slice(n_q + hd * HEAD_LANES, n_q + (hd + 1) * HEAD_LANES)
        pair, half = hd // 2, slice((hd % 2) * HEAD_LANES, (hd % 2 + 1) * HEAD_LANES)
        q_out[pair, :, half] = _head_norm_rope(q_pre[:, cols], q_pre[:, cols_b] * qb_tab, qa_tab)
        k_out[pair, :, half] = _head_norm_rope(k_pre[:, cols] + kr_a, k_rot, ka_tab)
        other(hd)


def _mixin(h, mod, layer, pw, tab, *, n_batch, lc, l):
    t_all, d = h.shape
    r = lc + l
    tm = r // 8 if (r % 128 == 0 and r // 8 >= 128) else r // 2
    tiles_per_batch = r // tm
    kern = functools.partial(_mixin_kernel, tm=tm, lc=lc, tiles_per_batch=tiles_per_batch)
    tok = lambda w: pl.BlockSpec((tm, w), lambda i: (i, 0))
    n_pairs = MLA_HEADS // 2
    qk_spec = pl.BlockSpec((n_pairs, tm, 2 * HEAD_LANES), lambda i: (0, i, 0))
    qk_shape = jax.ShapeDtypeStruct((n_pairs, t_all, 2 * HEAD_LANES), BF16)
    outs = [(MLA_HEADS * MLA_V, BF16), (BRANCH_W, F32), (GLA_HEADS * GLA_DK, F32),
            (GLA_HEADS * GLA_DK, F32), (GLA_HEADS * GLA_DV, BF16), (128, BF16),
            (GLA_HEADS * GLA_DV, F32)]
    return pl.pallas_call(
        kern,
        grid=(t_all // tm,),
        in_specs=[tok(d)] + _mod_specs(mod, layer, 1, tiles_per_batch, n_batch) + [
            _const_spec((1, d)), _const_spec((d, C_END)),
            _const_spec((1, MLA_Q_RANK)), _const_spec((MLA_Q_RANK, 2 * MLA_HEADS * HEAD_LANES)),
            _const_spec((1, MLA_KV_RANK)), _const_spec((MLA_KV_RANK, MLA_HEADS * HEAD_LANES)),
            _const_spec((MLA_KV_RANK, MLA_HEADS * MLA_V)),
            _const_spec((2, HEAD_LANES)), _const_spec((2, HEAD_LANES)),
            pl.BlockSpec((tm, HEAD_LANES), lambda i: (i % tiles_per_batch, 0)),
            pl.BlockSpec((tm, HEAD_LANES), lambda i: (i % tiles_per_batch, 0)),
        ],
        out_specs=[qk_spec, qk_spec] + [tok(w) for w, _ in outs],
        out_shape=[qk_shape, qk_shape] + [jax.ShapeDtypeStruct((t_all, w), dt) for w, dt in outs],
        compiler_params=_cparams(("arbitrary",)),
        name="mixer_in_proj",
    )(h, mod, mod, pw["g_mix"], pw["w_all"], pw["g_cq"], pw["w_uq"], pw["g_ckv"],
      pw["w_uk"], pw["w_uv"], pw["gq_tab"], pw["gk_tab"], tab[0], tab[1])


ONES_ROWS = 16
KEY_CHUNK = 256
SCORE_LOOKAHEAD = 5
BOUND_MARGIN = 1.02
MAX_SAFE_BOUND = 50.0
NORM_ROWS = 1088


def _pair_queries(q):
    qt = q.T
    row = lax.broadcasted_iota(jnp.int32, qt.shape, 0)
    zero = jnp.zeros_like(qt)
    return jnp.concatenate([jnp.where(row < HEAD_LANES, qt, zero),
                            jnp.where(row < HEAD_LANES, zero, qt)], axis=1)


def _score_bound(qbd, kmax_row):
    qf = qbd.astype(F32)
    qn2 = jnp.sum(qf * qf, axis=0, keepdims=True)
    return jnp.sqrt(qn2 * kmax_row) * BOUND_MARGIN


def _attn_tile(q_ref, k_ref, vt_ref, o_ref, hp, n_keys, bound=None):
    tq = q_ref.shape[1]
    vrows = slice(hp * 2 * MLA_V, (hp + 1) * 2 * MLA_V)
    qbd = _pair_queries(q_ref[hp])
    chunks = [(k0, min(KEY_CHUNK, n_keys - k0)) for k0 in range(0, n_keys, KEY_CHUNK)]

    def scores(k0, kn):
        return _dot(k_ref[hp, k0:k0 + kn, :], qbd)

    pending = [scores(*ch) for ch in chunks[:SCORE_LOOKAHEAD]]
    m = acc = None
    for ci, (k0, kn) in enumerate(chunks):
        if ci + SCORE_LOOKAHEAD < len(chunks):
            pending.append(scores(*chunks[ci + SCORE_LOOKAHEAD]))
        st = pending.pop(0)
        vt = jnp.concatenate([vt_ref[vrows, k0:k0 + kn], jnp.ones((ONES_ROWS, kn), BF16)], axis=0)
        if bound is not None:
            part = _dot(vt, jnp.exp2(st - bound).astype(BF16))
            acc = part if acc is None else acc + part
            continue
        cm = jnp.max(st, axis=0, keepdims=True)
        if m is None:
            m = cm
            acc = _dot(vt, jnp.exp2(st - m).astype(BF16))
        else:
            m_new = jnp.maximum(m, cm)
            acc = acc * jnp.exp2(m - m_new) + _dot(vt, jnp.exp2(st - m_new).astype(BF16))
            m = m_new
    inv = 1.0 / acc[2 * MLA_V:2 * MLA_V + 1]
    o_t = jnp.concatenate([acc[0:MLA_V, 0:tq] * inv[:, 0:tq],
                           acc[MLA_V:2 * MLA_V, tq:] * inv[:, tq:]], axis=0)
    o_ref[:, vrows] = o_t.T.astype(o_ref.dtype)


def _attn_kernel(q_ref, k_ref, v_ref, o_ref, vt_ref, kmax_ref, *, lc):
    r = k_ref.shape[1]
    tq = q_ref.shape[1]
    i = pl.program_id(1)
    pairs = range(MLA_HEADS // 2)

    @pl.when(i == 0)
    def _():
        vt_ref[...] = v_ref[...].T
        sel = jnp.where((lax.broadcasted_iota(jnp.int32, (2 * HEAD_LANES, 2 * tq), 0) < HEAD_LANES)
                        == (lax.broadcasted_iota(jnp.int32, (2 * HEAD_LANES, 2 * tq), 1) < tq),
                        1.0, 0.0).astype(BF16)
        for hp in pairs:
            kmax = None
            for r0 in range(0, r, NORM_ROWS):
                k = k_ref[hp, r0:min(r0 + NORM_ROWS, r), :]
                cur = jnp.max(_dot(k * k, sel), axis=0, keepdims=True)
                kmax = cur if kmax is None else jnp.maximum(kmax, cur)
            kmax_ref[hp] = jnp.broadcast_to(kmax, kmax_ref.shape[1:])

    if lc:
        @pl.when(i == 0)
        def _():
            for hp in pairs:
                _attn_tile(q_ref, k_ref, vt_ref, o_ref, hp, lc)

    bounds = [_score_bound(_pair_queries(q_ref[hp]), kmax_ref[hp][0:1]) for hp in pairs]
    worst = bounds[0]
    for bnd in bounds[1:]:
        worst = jnp.maximum(worst, bnd)
    safe = jnp.max(worst) <= MAX_SAFE_BOUND
    latent = (i > 0) if lc else (i >= 0)

    @pl.when(jnp.logical_and(latent, safe))
    def _():
        for hp in pairs:
            _attn_tile(q_ref, k_ref, vt_ref, o_ref, hp, r, bounds[hp])

    @pl.when(jnp.logical_and(latent, jnp.logical_not(safe)))
    def _():
        for hp in pairs:
            _attn_tile(q_ref, k_ref, vt_ref, o_ref, hp, r)


def _attention(q, k, v, *, n_batch, lc, l):
    n_pairs, t_all, _ = q.shape
    r = lc + l
    tq = lc
    tiles = r // tq
    kern = functools.partial(_attn_kernel, lc=lc)
    return pl.pallas_call(
        kern,
        grid=(n_batch, tiles),
        in_specs=[
            pl.BlockSpec((n_pairs, tq, 2 * HEAD_LANES), lambda b, i: (0, b * tiles + i, 0)),
            pl.BlockSpec((n_pairs, r, 2 * HEAD_LANES), lambda b, i: (0, b, 0)),
            pl.BlockSpec((r, MLA_HEADS * MLA_V), lambda b, i: (b, 0)),
        ],
        out_specs=pl.BlockSpec((tq, MLA_HEADS * MLA_V), lambda b, i: (b * tiles + i, 0)),
        out_shape=jax.ShapeDtypeStruct((t_all, MLA_HEADS * MLA_V), BF16),
        scratch_shapes=[pltpu.VMEM((MLA_HEADS * MLA_V, r), BF16),
                        pltpu.VMEM((n_pairs, 8, 2 * tq), F32)],
        compiler_params=_cparams(("arbitrary", "arbitrary")),
        name="mla_attention",
    )(q, k, v)


def _pool_kernel(p_ref, w_ref, s_ref, o_ref, *, lc, l):
    r = lc + l
    t = lax.broadcasted_iota(jnp.int32, (r, POOL_GDIM), 0)
    in_lat = t >= lc
    pos = jnp.where(in_lat, t - lc, t)
    seg_len = jnp.where(in_lat, l, lc)

    def shifted(x, dlt):
        src = pos + dlt
        rolled = pltpu.roll(x, (-dlt) % r, axis=0)
        return jnp.where(src >= 0, jnp.where(src < seg_len, rolled, 0.0), 0.0)

    for g, w in enumerate(POOL_WINDOWS):
        half = w // 2
        cols = slice(g * POOL_GDIM, (g + 1) * POOL_GDIM)
        x = p_ref[:, cols]
        right = x
        left = x
        step = 1
        while step < half:
            right = right + shifted(right, step)
            left = left + shifted(left, -step)
            step *= 2
        win = shifted(left, -1) + right
        cnt = (jnp.minimum(pos + half, seg_len) - jnp.maximum(pos - half, 0)).astype(F32)
        pooled = (win / cnt - x).astype(BF16)
        o_ref[:, cols] = (_dot(pooled, w_ref[g]) * s_ref[:, cols]).astype(o_ref.dtype)


def _pool(pz, w_pool, pool_scale, *, n_batch, lc, l):
    t_all = pz.shape[0]
    r = lc + l
    kern = functools.partial(_pool_kernel, lc=lc, l=l)
    return pl.pallas_call(
        kern,
        grid=(n_batch,),
        in_specs=[pl.BlockSpec((r, BRANCH_W), lambda b: (b, 0)),
                  _const_spec(w_pool.shape), _const_spec((1, BRANCH_W))],
        out_specs=pl.BlockSpec((r, BRANCH_W), lambda b: (b, 0)),
        out_shape=jax.ShapeDtypeStruct((t_all, BRANCH_W), BF16),
        compiler_params=_cparams(("arbitrary",)),
        name="multiscale_pool",
    )(pz, w_pool, pool_scale)


def _gla_kernel(q_ref, k_ref, v_ref, ga_ref, gr_ref, wa_ref, ba_ref, go_ref, o_ref,
                qt_s, kts_s, ket_s, dec_s, vbd_s, og_s, *, lc, l, blk):
    r = lc + l
    c = GLA_CHUNK
    n_chunks = r // c
    n_ctx = lc // c
    cpb = blk // c
    w = 2 * GLA_DK
    bi_ = lax.broadcasted_iota(jnp.int32, (blk, blk), 0)
    bj_ = lax.broadcasted_iota(jnp.int32, (blk, blk), 1)
    same_chunk = (bi_ // c) == (bj_ // c)
    ones_m = jnp.where(same_chunk, 1.0, 0.0).astype(BF16)
    ind = jnp.where(lax.broadcasted_iota(jnp.int32, (blk, cpb * w), 0) // c
                    == lax.broadcasted_iota(jnp.int32, (blk, cpb * w), 1) // w, 1.0, 0.0).astype(BF16)
    head0_k = lax.broadcasted_iota(jnp.int32, (blk, w), 1) < GLA_DK
    head0_v = lax.broadcasted_iota(jnp.int32, (blk, 2 * GLA_DV), 1) < GLA_DV
    ci_ = lax.broadcasted_iota(jnp.int32, (c, w), 0)
    cj_ = lax.broadcasted_iota(jnp.int32, (c, w), 1) % c
    state_diag = ((lax.broadcasted_iota(jnp.int32, (w, 2 * GLA_DV), 0) < GLA_DK)
                  == (lax.broadcasted_iota(jnp.int32, (w, 2 * GLA_DV), 1) < GLA_DV))

    def split_store(dst, b, x0, x1):
        for cc in range(cpb):
            base = (b * cpb + cc) * 2 * c
            dst[pl.ds(pl.multiple_of(base, c), c), :] = x0[cc * c:(cc + 1) * c]
            dst[pl.ds(pl.multiple_of(base + c, c), c), :] = x1[cc * c:(cc + 1) * c]

    def vprep(b, carry):
        v = v_ref[pl.ds(pl.multiple_of(b * blk, blk), blk), :]
        zero = jnp.zeros_like(v)
        split_store(vbd_s, b, jnp.where(head0_v, v, zero), jnp.where(head0_v, zero, v))
        return carry

    lax.fori_loop(0, r // blk, vprep, 0)
    og_s[...] = jnp.zeros_like(og_s)

    dirs = (0, 1)
    tri_m = [jnp.where(same_chunk, jnp.where((bj_ <= bi_) if d == 0 else (bj_ >= bi_), 1.0, 0.0),
                       0.0).astype(BF16) for d in dirs]
    causal = [(cj_ <= ci_) if d == 0 else (cj_ >= ci_) for d in dirs]

    def prep(blocks):
        block_of = dict(enumerate(blocks))
        blocks = list(block_of)
        jobs = [(b, d) for b in blocks for d in dirs]
        rows = {b: pl.ds(pl.multiple_of(block_of[b] * blk, blk), blk) for b in blocks}
        ga = {b: ga_ref[rows[b], :] for b in blocks}
        logit = {j: _dot(ga[j[0]], wa_ref[j[1]]) + ba_ref[j[1]] for j in jobs}
        la = {j: (jnp.minimum(x, 0.0) - jnp.log(1.0 + jnp.exp(-jnp.abs(x)))) * (1.0 / GLA_TAU)
              for j, x in logit.items()}
        hl = {}
        for j in jobs:
            hi = la[j].astype(BF16)
            hl[j] = jnp.concatenate([hi, (la[j] - hi.astype(F32)).astype(BF16)], axis=1)
        cs = {j: _dot(tri_m[j[1]], hl[j]) for j in jobs}
        ts = {j: _dot(ones_m, hl[j]) for j in jobs}
        totb = {j: _dot_tn(hl[j], ind) for j in jobs}
        q = {b: q_ref[rows[b], :] * (GLA_DK ** -0.5) for b in blocks}
        k = {b: k_ref[rows[b], :] for b in blocks}
        for j in jobs:
            b, d = j
            first_chunk = block_of[b] * cpb
            bcum = cs[j][:, :w] + cs[j][:, w:]
            tot = ts[j][:, :w] + ts[j][:, w:]
            for cc in range(cpb):
                dec_s[d, first_chunk + cc] = jnp.exp(totb[j][:w, cc * w:(cc + 1) * w]
                                                 + totb[j][w:, cc * w:(cc + 1) * w])
            qt_s[d, rows[b], :] = (q[b] * jnp.exp(bcum)).astype(BF16)
            ke = (k[b] * jnp.exp(tot - bcum)).astype(BF16)
            kt = (k[b] * jnp.exp(-bcum)).astype(BF16)
            zero = jnp.zeros_like(kt)
            kt0, kt1 = jnp.where(head0_k, kt, zero), jnp.where(head0_k, zero, kt)
            for cc in range(cpb):
                crow = slice(cc * c, (cc + 1) * c)
                dst_rows = pl.ds(pl.multiple_of((first_chunk + cc) * 2 * c, 2 * c), 2 * c)
                kts_s[d, dst_rows, :] = jnp.concatenate([kt0[crow], kt1[crow]], axis=0).T
                ket_s[d, dst_rows, :] = ke[crow].T

    n_blocks = r // blk

    def prep_pair(t, carry):
        prep([2 * t, 2 * t + 1])
        return carry

    lax.fori_loop(0, n_blocks // 2, prep_pair, 0)
    if n_blocks % 2:
        prep([n_blocks - 1])

    def scan(t, sts):
        work = []
        for u in range(GLA_UNROLL):
            i = t * GLA_UNROLL + u
            for d in dirs:
                if d == 0:
                    n = i
                else:
                    n = jnp.where(i < n_ctx, n_ctx - 1 - i, n_chunks - 1 - (i - n_ctx))
                rows = pl.ds(pl.multiple_of(n * c, c), c)
                rows2 = pl.ds(pl.multiple_of(n * 2 * c, 2 * c), 2 * c)
                qc = qt_s[d, rows, :]
                a = jnp.where(causal[d], _dot(qc, kts_s[d, rows2, :]), 0.0).astype(BF16)
                dst = jnp.where(state_diag, _dot(ket_s[d, rows2, :], v_ref[rows, :]), 0.0)
                work.append((d, n, rows, rows2, qc, a, dst))
        sts = list(sts)
        for d, n, rows, rows2, qc, a, dst in work:
            o = _dot(jnp.concatenate([a, qc], axis=1),
                     jnp.concatenate([vbd_s[rows2, :], sts[d].astype(BF16)], axis=0))
            og_s[rows, :] = og_s[rows, :] + o
            dec = dec_s[d, n]
            sts[d] = sts[d] * jnp.concatenate([dec, dec], axis=1) + dst
        return tuple(sts)

    zero_state = jnp.zeros((w, 2 * GLA_DV), F32)
    lax.fori_loop(0, n_chunks // GLA_UNROLL, scan, (zero_state, zero_state))

    for hh in range(2):
        vc = slice(hh * GLA_DV, (hh + 1) * GLA_DV)
        on = _rms(og_s[:, vc], go_ref[...])
        o_ref[:, vc] = (on * _silu(gr_ref[:, vc])).astype(o_ref.dtype)


def _gla(gq, gk, gv, ga, gr, wa, ba, g_o, *, n_batch, lc, l):
    t_all = gq.shape[0]
    r = lc + l
    blk = 256 if r % 256 == 0 else 128
    kern = functools.partial(_gla_kernel, lc=lc, l=l, blk=blk)
    seq = lambda w: pl.BlockSpec((r, w), lambda b, hp: (b, hp), pipeline_mode=pl.Buffered(1))
    return pl.pallas_call(
        kern,
        grid=(n_batch, GLA_HEADS // 2),
        in_specs=[seq(2 * GLA_DK), seq(2 * GLA_DK), seq(2 * GLA_DV),
                  pl.BlockSpec((r, 128), lambda b, hp: (b, 0), pipeline_mode=pl.Buffered(1)),
                  seq(2 * GLA_DV),
                  pl.BlockSpec((2, 128, 2 * GLA_DK), lambda b, hp: (0, 0, hp)),
                  pl.BlockSpec((2, 1, 2 * GLA_DK), lambda b, hp: (0, 0, hp)),
                  pl.BlockSpec((1, GLA_DV), lambda b, hp: (0, 0))],
        out_specs=pl.BlockSpec((r, 2 * GLA_DV), lambda b, hp: (b, hp)),
        out_shape=jax.ShapeDtypeStruct((t_all, GLA_HEADS * GLA_DV), BF16),
        scratch_shapes=[pltpu.VMEM((2, r, 2 * GLA_DK), BF16), pltpu.VMEM((2, 2 * r, 2 * GLA_DK), BF16),
                        pltpu.VMEM((2, 2 * r, GLA_CHUNK), BF16),
                        pltpu.VMEM((2, r // GLA_CHUNK, 2 * GLA_DK, 2 * GLA_DK), F32),
                        pltpu.VMEM((2 * r, 2 * GLA_DV), BF16), pltpu.VMEM((r, 2 * GLA_DV), F32)],
        compiler_params=_cparams(("arbitrary", "arbitrary")),
        name="gla_bidir",
    )(gq, gk, gv, ga, gr, wa, ba, g_o)


def _merge_kernel(h_ref, modb_ref, modc_ref, g_ref, oa_ref, op_ref, og_ref, wg_ref, wb_ref,
                  wo_ref, o_ref, *, tm, lc, tiles_per_batch):
    d = h_ref.shape[-1]
    j = pl.program_id(0) % tiles_per_batch
    shift, scale, gate = _mod_rows(modb_ref, modc_ref, j, tm, d, lc)
    h = h_ref[...]
    u = (_rms(h, g_ref[...]) * (1.0 + scale) + shift).astype(BF16)
    m = None
    for br, src in enumerate((oa_ref, op_ref, og_ref)):
        gt = _sigmoid(_dot(u, wg_ref[:, br * d:(br + 1) * d]))
        term = gt * _dot(src[...], wb_ref[br])
        m = term if m is None else m + term
    y = _dot(m.astype(BF16), wo_ref[...])
    o_ref[...] = h + gate * y


def _merge(h, mod, layer, g_mix, o_a, o_p, o_g, w_gates, w_branch, w_out, *, n_batch, lc, l):
    t_all, d = h.shape
    r = lc + l
    tm = r // 8 if (r % 128 == 0 and r // 8 >= 128) else r // 2
    tiles_per_batch = r // tm
    kern = functools.partial(_merge_kernel, tm=tm, lc=lc, tiles_per_batch=tiles_per_batch)
    tok = lambda w: pl.BlockSpec((tm, w), lambda i: (i, 0))
    return pl.pallas_call(
        kern,
        grid=(t_all // tm,),
        in_specs=[tok(d)] + _mod_specs(mod, layer, 1, tiles_per_batch, n_batch) + [
            _const_spec((1, d)), tok(BRANCH_W), tok(BRANCH_W), tok(BRANCH_W),
            _const_spec(w_gates.shape), _const_spec(w_branch.shape), _const_spec(w_out.shape)],
        out_specs=tok(d),
        out_shape=jax.ShapeDtypeStruct((t_all, d), F32),
        compiler_params=_cparams(("arbitrary",)),
        name="branch_merge",
    )(h, mod, mod, g_mix, o_a, o_p, o_g, w_gates, w_branch, w_out)


_ROPE_PARTNER = np.arange(MLA_ROPE) ^ (AXIS_DIM // 2)


def _rope_table(lc, l):
    pos = jnp.arange(l)
    row = (pos // GRID_W).astype(F32)
    col = (pos % GRID_W).astype(F32)
    inv = ROPE_THETA ** (-jnp.arange(0, AXIS_DIM, 2, dtype=F32) / AXIS_DIM)
    ang = jnp.stack([row, col], axis=-1)[:, :, None] * inv
    cos = jnp.repeat(jnp.cos(ang)[:, :, None, :], 2, axis=2).reshape(l, MLA_ROPE)
    sin = jnp.repeat(jnp.sin(ang)[:, :, None, :], 2, axis=2)
    sin = (sin * jnp.array([-1.0, 1.0], F32)[None, None, :, None]).reshape(l, MLA_ROPE)
    pad = HEAD_LANES - MLA_QK
    tab_c = jnp.concatenate([
        jnp.concatenate([jnp.ones((lc, MLA_QK), F32), jnp.zeros((lc, pad), F32)], axis=1),
        jnp.concatenate([jnp.ones((l, MLA_NOPE), F32), cos, jnp.zeros((l, pad), F32)], axis=1)],
        axis=0)
    tab_s = jnp.concatenate([
        jnp.zeros((lc, HEAD_LANES), F32),
        jnp.concatenate([jnp.zeros((l, MLA_NOPE), F32), sin, jnp.zeros((l, pad), F32)], axis=1)],
        axis=0)
    return tab_c, tab_s


def _pack_layer(p, layer):
    d = p["w_in"].shape[1]
    w_in = p["w_in"][layer]
    offs = np.cumsum([0, MLA_Q_RANK, MLA_KV_RANK, MLA_ROPE, BRANCH_W, GLA_HEADS * GLA_DK,
                      GLA_HEADS * GLA_DK, GLA_HEADS * GLA_DV, 2 * GLA_GATE_RANK,
                      GLA_HEADS * GLA_DV, N_BRANCH * d])
    cq, ckv, kr, pz, gq, gk, gv, ga, gr, gates = [w_in[:, offs[i]:offs[i + 1]] for i in range(10)]
    pad = HEAD_LANES - MLA_QK

    def rope_lanes(x):
        lead = x.shape[:-1]
        return jnp.concatenate([jnp.zeros(lead + (MLA_NOPE,), F32), x, jnp.zeros(lead + (pad,), F32)],
                               axis=-1)

    ga_ext = jnp.concatenate([ga, jnp.zeros((d, 128 - 2 * GLA_GATE_RANK), F32)], axis=1)
    w_all = jnp.concatenate([cq, ckv, rope_lanes(kr), rope_lanes(kr[:, _ROPE_PARTNER]),
                             pz, gq, gk, gv, ga_ext, gr], axis=1).astype(BF16)

    w_uq = p["w_uq"][layer].reshape(MLA_Q_RANK, MLA_HEADS, MLA_QK)
    w_uq_a = jnp.concatenate([w_uq, jnp.zeros((MLA_Q_RANK, MLA_HEADS, pad), F32)], axis=2)
    w_uq_b = rope_lanes(w_uq[:, :, MLA_NOPE + _ROPE_PARTNER])
    w_uq = jnp.concatenate([w_uq_a.reshape(MLA_Q_RANK, -1), w_uq_b.reshape(MLA_Q_RANK, -1)],
                           axis=1).astype(BF16)
    w_ukv = p["w_ukv"][layer].reshape(MLA_KV_RANK, MLA_HEADS, MLA_NOPE + MLA_V)
    w_uk = jnp.concatenate([w_ukv[:, :, :MLA_NOPE],
                            jnp.zeros((MLA_KV_RANK, MLA_HEADS, HEAD_LANES - MLA_NOPE), F32)], axis=2)
    w_uk = w_uk.reshape(MLA_KV_RANK, MLA_HEADS * HEAD_LANES).astype(BF16)
    w_uv = w_ukv[:, :, MLA_NOPE:].reshape(MLA_KV_RANK, MLA_HEADS * MLA_V).astype(BF16)

    def gain_tab(g):
        return jnp.stack([jnp.concatenate([g, jnp.zeros((pad,), F32)]),
                          rope_lanes(g[MLA_NOPE + _ROPE_PARTNER])])

    wa = jnp.zeros((2, 128, GLA_HEADS * GLA_DK), F32)
    for dr in range(2):
        wa = wa.at[dr, dr * GLA_GATE_RANK:(dr + 1) * GLA_GATE_RANK].set(p["w_a2"][layer, dr])

    return dict(
        g_mix=p["g_mix"][layer][None, :], w_all=w_all,
        g_cq=p["g_cq"][layer][None, :], w_uq=w_uq,
        g_ckv=p["g_ckv"][layer][None, :], w_uk=w_uk, w_uv=w_uv,
        gq_tab=gain_tab(p["g_qn"][layer]) * (MLA_QK ** -0.5 * math.log2(math.e)), gk_tab=gain_tab(p["g_kn"][layer]),
        w_gates=gates.astype(BF16),
        w_pool=p["w_pool"][layer].astype(BF16), pool_scale=p["pool_scale"][layer][None, :],
        wa=wa.astype(BF16), ba=p["b_a2"][layer][:, None, :], g_gla_o=p["g_gla_o"][layer][None, :],
        w_branch=p["w_branch"][layer].astype(BF16), w_out=p["w_out"][layer].astype(BF16),
    )


def kernel(x, c, ctx, c_ctx, w_ada, b_ada, g_ffn1, ffn1_w1, ffn1_w3, ffn1_w2, g_mix, w_in,
           g_cq, w_uq, g_ckv, w_ukv, g_qn, g_kn, w_pool, pool_scale, w_a2, b_a2, g_gla_o,
           w_branch, w_out, g_ffn2, ffn2_w1, ffn2_w3, ffn2_w2):
    n_batch, l, d = x.shape
    lc = ctx.shape[1]
    depth = w_ada.shape[0]
    dims = dict(n_batch=n_batch, lc=lc, l=l)
    p = dict(w_in=w_in, g_mix=g_mix, g_cq=g_cq, w_uq=w_uq, g_ckv=g_ckv, w_ukv=w_ukv, g_qn=g_qn,
             g_kn=g_kn, w_pool=w_pool, pool_scale=pool_scale, w_a2=w_a2, b_a2=b_a2,
             g_gla_o=g_gla_o, w_branch=w_branch, w_out=w_out)

    cond_rows = 16
    cond = jnp.zeros((cond_rows, d), F32).at[:n_batch].set(c).at[n_batch].set(c_ctx)
    mod = _ada_mod(cond, w_ada, b_ada).reshape(depth, cond_rows, 3, 3, d)

    tab = _rope_table(lc, l)
    h = jnp.concatenate([ctx, x], axis=1).reshape(n_batch * (lc + l), d)
    for layer in range(depth):
        last = layer == depth - 1
        pw = _pack_layer(p, layer)
        h = _ffn(h, mod, layer, 0, g_ffn1[layer], ffn1_w1[layer].astype(BF16),
                 ffn1_w3[layer].astype(BF16), ffn1_w2[layer].astype(BF16), **dims)
        q, k, v, pz, gq, gk, gv, ga, gr = _mixin(h, mod, layer, pw, tab, **dims)
        o_a = _attention(q, k, v, **dims)
        o_p = _pool(pz, pw["w_pool"], pw["pool_scale"], **dims)
        o_g = _gla(gq, gk, gv, ga, gr, pw["wa"], pw["ba"], pw["g_gla_o"], **dims)
        h = _merge(h, mod, layer, pw["g_mix"], o_a, o_p, o_g, pw["w_gates"], pw["w_branch"],
                   pw["w_out"], **dims)
        h = _ffn(h, mod, layer, 2, g_ffn2[layer], ffn2_w1[layer].astype(BF16),
                 ffn2_w3[layer].astype(BF16), ffn2_w2[layer].astype(BF16), lat_only=last, **dims)
    return h.reshape(n_batch, l, d)
```

```python
import functools
import math

import jax
import jax.numpy as jnp
import numpy as np
from jax import lax
from jax.experimental import pallas as pl
from jax.experimental.pallas import tpu as pltpu

F32 = jnp.float32
BF16 = jnp.bfloat16

EPS = 1e-6
GRID_W = 64
N_MOD = 9
MLA_HEADS = 8
MLA_Q_RANK = 256
MLA_KV_RANK = 128
MLA_NOPE = 64
MLA_ROPE = 32
MLA_V = 64
MLA_QK = MLA_NOPE + MLA_ROPE
AXIS_DIM = MLA_ROPE // 2
ROPE_THETA = 10000.0
POOL_WINDOWS = (2, 4, 8, 16)
BRANCH_W = 512
POOL_GDIM = BRANCH_W // len(POOL_WINDOWS)
GLA_HEADS = 4
GLA_DK = 64
GLA_DV = 128
GLA_GATE_RANK = 16
GLA_TAU = 16.0
GLA_CHUNK = 64
GLA_UNROLL = 4
N_BRANCH = 3
HEAD_LANES = 128

C_CQ, C_CKV, C_KRA, C_KRB, C_PZ, C_GQ, C_GK, C_GV, C_GA, C_GR, C_END = (
    0, 256, 384, 512, 640, 1152, 1408, 1664, 2176, 2304, 2816)

VMEM_LIMIT = 56 * 1024 * 1024
MXU_ROW_GROUP = 64
CAST_ROWS = 256


def _cparams(sem):
    return pltpu.CompilerParams(dimension_semantics=sem, vmem_limit_bytes=VMEM_LIMIT)


def _const_spec(shape):
    nd = len(shape)
    return pl.BlockSpec(shape, lambda *_: (0,) * nd, pipeline_mode=pl.Buffered(1))


def _dot(a, b):
    return jnp.dot(a, b, preferred_element_type=F32)


def _dot_nt(a, b):
    return lax.dot_general(a, b, (((1,), (1,)), ((), ())), preferred_element_type=F32)


def _dot_tn(a, b):
    return lax.dot_general(a, b, (((0,), (0,)), ((), ())), preferred_element_type=F32)


def _silu(x):
    return x * (1.0 / (1.0 + jnp.exp(-x)))


def _sigmoid(x):
    return 1.0 / (1.0 + jnp.exp(-x))


def _rms(x, g):
    ms = jnp.mean(x * x, axis=-1, keepdims=True)
    return x * lax.rsqrt(ms + EPS) * g


def _ada_kernel(c_ref, w_ref, b_ref, o_ref):
    s = _silu(c_ref[...]).astype(BF16)
    o_ref[...] = _dot(s, w_ref[...].astype(BF16)) + b_ref[...]


def _ada_mod(cond, w_ada, b_ada):
    depth, d, n = w_ada.shape
    rows = cond.shape[0]
    tn = 1024
    return pl.pallas_call(
        _ada_kernel,
        grid=(depth, n // tn),
        in_specs=[
            pl.BlockSpec((rows, d), lambda l, j: (0, 0)),
            pl.BlockSpec((None, d, tn), lambda l, j: (l, 0, j)),
            pl.BlockSpec((None, 1, tn), lambda l, j: (l, 0, j)),
        ],
        out_specs=pl.BlockSpec((None, rows, tn), lambda l, j: (l, 0, j)),
        out_shape=jax.ShapeDtypeStruct((depth, rows, n), F32),
        compiler_params=_cparams(("arbitrary", "arbitrary")),
        name="ada_mod",
    )(cond, w_ada, b_ada.reshape(depth, 1, n))


def _cast_kernel(*refs):
    n = len(refs) // 2
    for src, dst in zip(refs[:n], refs[n:]):
        dst[...] = src[...].astype(BF16)


def _cast_bf16(*ws):
    depth, rows, cols = ws[0].shape
    tr = CAST_ROWS if rows % CAST_ROWS == 0 else rows
    spec = pl.BlockSpec((None, tr, cols), lambda l, i: (l, i, 0))
    return pl.pallas_call(
        _cast_kernel,
        grid=(depth, rows // tr),
        in_specs=[spec] * len(ws),
        out_specs=[spec] * len(ws),
        out_shape=[jax.ShapeDtypeStruct(w.shape, BF16) for w in ws],
        compiler_params=_cparams(("arbitrary", "arbitrary")),
        name="cast_weights",
    )(*ws)


def _mod_rows(modb_ref, modc_ref, tile_in_batch, tm, d, lc):
    modb = modb_ref[...]
    modc = modc_ref[...]
    if lc == 0:
        return modb[0:1], modb[1:2], modb[2:3]
    row = tile_in_batch * tm + lax.broadcasted_iota(jnp.int32, (tm, d), 0)
    is_ctx = row < lc
    return tuple(jnp.where(is_ctx, modc[i:i + 1], modb[i:i + 1]) for i in range(3))


def _mod_specs(mod, layer, group, tiles_per_batch, n_batch):
    d = mod.shape[-1]
    blk = (None, None, None, 3, d)
    return [
        pl.BlockSpec(blk, lambda i: (layer, i // tiles_per_batch, group, 0, 0)),
        pl.BlockSpec(blk, lambda i: (layer, n_batch, group, 0, 0)),
    ]


def _ffn_kernel(h_ref, c_ref, modb_ref, modc_ref, g_ref, w1_ref, w3_ref, w2_ref, o_ref, hid_ref,
                *, tm, lc, tiles_per_batch, col_chunk, ctx_tiles):
    d = h_ref.shape[-1]
    j = pl.program_id(0) % tiles_per_batch
    shift, scale, gate = _mod_rows(modb_ref, modc_ref, j, tm, d, lc)

    def read(rows):
        if ctx_tiles:
            return jnp.where(j < ctx_tiles, c_ref[rows, :], h_ref[rows, :])
        return h_ref[rows, :]
    if tm >= 8 * MXU_ROW_GROUP and tm % MXU_ROW_GROUP == 0:
        cut = (tm // MXU_ROW_GROUP + 1) // 2 * MXU_ROW_GROUP
        groups = [slice(0, cut), slice(cut, tm)]
    else:
        groups = [slice(0, tm)]

    def rows_of(x, rows):
        return x if x.shape[0] == 1 else x[rows]

    u = [(_rms(read(rows), g_ref[...]) * (1.0 + rows_of(scale, rows))
          + rows_of(shift, rows)).astype(BF16) for rows in groups]
    d_ff = w1_ref.shape[-1]
    for c0 in range(0, d_ff, col_chunk):
        cols = slice(c0, min(c0 + col_chunk, d_ff))
        for rows, ug in zip(groups, u):
            a = _dot(ug, w1_ref[:, cols])
            b = _dot(ug, w3_ref[:, cols])
            hid_ref[rows, cols] = (_silu(a) * b).astype(BF16)
    for rows in groups:
        y = _dot(hid_ref[rows, :], w2_ref[...])
        o_ref[rows, :] = read(rows) + 0.5 * rows_of(gate, rows) * y


def _ffn(h, mod, layer, group, g, w1, w3, w2, *, n_batch, lc, l, lat_only=False, ctx_src=None):
    d = h.shape[-1]
    t_all = n_batch * (lc + l)
    d_ff = w1.shape[-1]
    r = lc + l
    ctx_tiles = 0
    ctx_spec = pl.BlockSpec((8, d), lambda i: (0, 0))
    ctx_arr = h
    if ctx_src is not None:
        tm = min(math.gcd(lc, l), 256)
        tiles_per_batch = r // tm
        ctx_tiles = lc // tm
        lat_tiles = l // tm
        rows_out = t_all
        in_map = lambda i: ((i // tiles_per_batch) * lat_tiles
                            + jnp.maximum(i % tiles_per_batch - ctx_tiles, 0), 0)
        ctx_spec = pl.BlockSpec((tm, d), lambda i: ((i // tiles_per_batch) * ctx_tiles
                                                    + jnp.minimum(i % tiles_per_batch, ctx_tiles - 1), 0))
        ctx_arr = ctx_src
        lc_k = lc
    elif lat_only:
        tm = math.gcd(lc, l) if lc else l
        tm = min(tm, 256)
        tiles_per_batch = l // tm
        skip = lc // tm
        rows_out = n_batch * l
        in_map = lambda i: ((i // tiles_per_batch) * (r // tm) + skip + i % tiles_per_batch, 0)
        lc_k = 0
    else:
        tm = r // 4
        tiles_per_batch = r // tm
        rows_out = t_all
        in_map = lambda i: (i, 0)
        lc_k = lc
    col_chunk = min(d_ff, 512)
    kern = functools.partial(_ffn_kernel, tm=tm, lc=lc_k, tiles_per_batch=tiles_per_batch,
                             col_chunk=col_chunk, ctx_tiles=ctx_tiles)
    return pl.pallas_call(
        kern,
        grid=(rows_out // tm,),
        in_specs=[pl.BlockSpec((tm, d), in_map), ctx_spec]
        + _mod_specs(mod, layer, group, tiles_per_batch, n_batch)
        + [_const_spec((1, d))]
        + [pl.BlockSpec((None,) + w.shape[1:], lambda i: (layer, 0, 0), pipeline_mode=pl.Buffered(1))
           for w in (w1, w3, w2)],
        out_specs=pl.BlockSpec((tm, d), lambda i: (i, 0)),
        out_shape=jax.ShapeDtypeStruct((rows_out, d), F32),
        scratch_shapes=[pltpu.VMEM((tm, d_ff), BF16)],
        compiler_params=_cparams(("arbitrary",)),
        name="ffn_half_step",
    )(h, ctx_arr, mod, mod, g.reshape(1, d), w1, w3, w2)


def _head_norm_rope(xa, rot, tab_a):
    ss = jnp.sum(xa * xa, axis=-1, keepdims=True) * (1.0 / MLA_QK)
    return ((xa * tab_a + rot) * lax.rsqrt(ss + EPS)).astype(BF16)


def _mixin_kernel(h_ref, modb_ref, modc_ref, g_ref, w_ref, gcq_ref, wuq_ref, gckv_ref,
                  wuk_ref, wuv_ref, gq_ref, gk_ref, tabc_ref, tabs_ref,
                  q_out, k_out, v_out, pz_out, gq_out, gk_out, gv_out, ga_out, gr_out,
                  *, tm, lc, tiles_per_batch):
    d = h_ref.shape[-1]
    j = pl.program_id(0) % tiles_per_batch
    shift, scale, _ = _mod_rows(modb_ref, modc_ref, j, tm, d, lc)
    u = (_rms(h_ref[...], g_ref[...]) * (1.0 + scale) + shift).astype(BF16)

    lowrank = _dot(u, w_ref[:, C_CQ:C_PZ])
    cqn = _rms(lowrank[:, C_CQ:C_CKV], gcq_ref[...]).astype(BF16)
    ckvn = _rms(lowrank[:, C_CKV:C_KRA], gckv_ref[...]).astype(BF16)
    kr_a = lowrank[:, C_KRA:C_KRB]
    kr_b = lowrank[:, C_KRB:C_PZ]
    q_pre = _dot(cqn, wuq_ref[...])
    k_pre = _dot(ckvn, wuk_ref[...])
    v_out[...] = _dot(ckvn, wuv_ref[...]).astype(BF16)

    tab_c = tabc_ref[...]
    tab_s = tabs_ref[...]
    gq = gq_ref[...]
    gk = gk_ref[...]
    qa_tab, qb_tab = tab_c * gq[0:1], tab_s * gq[1:2]
    ka_tab = tab_c * gk[0:1]
    k_rot = kr_b * (tab_s * gk[1:2])
    n_q = MLA_HEADS * HEAD_LANES

    def other(idx):
        if idx == 0:
            pz_out[...] = _dot(u, w_ref[:, C_PZ:C_GQ])
        elif idx == 1:
            gq_out[...] = _dot(u, w_ref[:, C_GQ:C_GK])
        elif idx == 2:
            gk_out[...] = _dot(u, w_ref[:, C_GK:C_GV])
        elif idx == 3:
            gv_out[...] = _dot(u, w_ref[:, C_GV:C_GA]).astype(BF16)
        elif idx == 4:
            ga_out[...] = _dot(u, w_ref[:, C_GA:C_GR]).astype(BF16)
        elif idx == 5:
            gr_out[...] = _dot(u, w_ref[:, C_GR:C_END])

    for hd in range(MLA_HEADS):
        cols = slice(hd * HEAD_LANES, (hd + 1) * HEAD_LANES)
        cols_b = slice(n_q + hd * HEAD_LANES, n_q + (hd + 1) * HEAD_LANES)
        pair, half = hd // 2, slice((hd % 2) * HEAD_LANES, (hd % 2 + 1) * HEAD_LANES)
        q_out[pair, :, half] = _head_norm_rope(q_pre[:, cols], q_pre[:, cols_b] * qb_tab, qa_tab)
        k_out[pair, :, half] = _head_norm_rope(k_pre[:, cols] + kr_a, k_rot, ka_tab)
        other(hd)


def _mixin(h, mod, layer, pw, tab, *, n_batch, lc, l):
    t_all, d = h.shape
    r = lc + l
    tm = r // 8 if (r % 128 == 0 and r // 8 >= 128) else r // 2
    tiles_per_batch = r // tm
    kern = functools.partial(_mixin_kernel, tm=tm, lc=lc, tiles_per_batch=tiles_per_batch)
    tok = lambda w: pl.BlockSpec((tm, w), lambda i: (i, 0))
    n_pairs = MLA_HEADS // 2
    qk_spec = pl.BlockSpec((n_pairs, tm, 2 * HEAD_LANES), lambda i: (0, i, 0))
    qk_shape = jax.ShapeDtypeStruct((n_pairs, t_all, 2 * HEAD_LANES), BF16)
    outs = [(MLA_HEADS * MLA_V, BF16), (BRANCH_W, F32), (GLA_HEADS * GLA_DK, F32),
            (GLA_HEADS * GLA_DK, F32), (GLA_HEADS * GLA_DV, BF16), (128, BF16),
            (GLA_HEADS * GLA_DV, F32)]
    return pl.pallas_call(
        kern,
        grid=(t_all // tm,),
        in_specs=[tok(d)] + _mod_specs(mod, layer, 1, tiles_per_batch, n_batch) + [
            _const_spec((1, d)), _const_spec((d, C_END)),
            _const_spec((1, MLA_Q_RANK)), _const_spec((MLA_Q_RANK, 2 * MLA_HEADS * HEAD_LANES)),
            _const_spec((1, MLA_KV_RANK)), _const_spec((MLA_KV_RANK, MLA_HEADS * HEAD_LANES)),
            _const_spec((MLA_KV_RANK, MLA_HEADS * MLA_V)),
            _const_spec((2, HEAD_LANES)), _const_spec((2, HEAD_LANES)),
            pl.BlockSpec((tm, HEAD_LANES), lambda i: (i % tiles_per_batch, 0)),
            pl.BlockSpec((tm, HEAD_LANES), lambda i: (i % tiles_per_batch, 0)),
        ],
        out_specs=[qk_spec, qk_spec] + [tok(w) for w, _ in outs],
        out_shape=[qk_shape, qk_shape] + [jax.ShapeDtypeStruct((t_all, w), dt) for w, dt in outs],
        compiler_params=_cparams(("arbitrary",)),
        name="mixer_in_proj",
    )(h, mod, mod, pw["g_mix"], pw["w_all"], pw["g_cq"], pw["w_uq"], pw["g_ckv"],
      pw["w_uk"], pw["w_uv"], pw["gq_tab"], pw["gk_tab"], tab[0], tab[1])


ONES_ROWS = 16
KEY_CHUNK = 256
SCORE_LOOKAHEAD = 5
BOUND_MARGIN = 1.02
MAX_SAFE_BOUND = 50.0
NORM_ROWS = 1088


def _pair_queries(q):
    qt = q.T
    row = lax.broadcasted_iota(jnp.int32, qt.shape, 0)
    zero = jnp.zeros_like(qt)
    return jnp.concatenate([jnp.where(row < HEAD_LANES, qt, zero),
                            jnp.where(row < HEAD_LANES, zero, qt)], axis=1)


def _score_bound(qbd, kmax_row):
    qf = qbd.astype(F32)
    qn2 = jnp.sum(qf * qf, axis=0, keepdims=True)
    return jnp.sqrt(qn2 * kmax_row) * BOUND_MARGIN


def _attn_tile(q_ref, k_ref, vt_ref, o_ref, hp, n_keys, bound=None):
    tq = q_ref.shape[1]
    vrows = slice(hp * 2 * MLA_V, (hp + 1) * 2 * MLA_V)
    qbd = _pair_queries(q_ref[hp])
    chunks = [(k0, min(KEY_CHUNK, n_keys - k0)) for k0 in range(0, n_keys, KEY_CHUNK)]

    def scores(k0, kn):
        return _dot(k_ref[hp, k0:k0 + kn, :], qbd)

    pending = [scores(*ch) for ch in chunks[:SCORE_LOOKAHEAD]]
    m = acc = psum = None
    for ci, (k0, kn) in enumerate(chunks):
        if ci + SCORE_LOOKAHEAD < len(chunks):
            pending.append(scores(*chunks[ci + SCORE_LOOKAHEAD]))
        st = pending.pop(0)
        if bound is not None:
            e = jnp.exp2(st - bound)
            part_sum = jnp.sum(e.reshape(kn // 8, 8, 2 * tq), axis=0)
            part = _dot(vt_ref[vrows, k0:k0 + kn], e.astype(BF16))
            acc = part if acc is None else acc + part
            psum = part_sum if psum is None else psum + part_sum
            continue
        vt = jnp.concatenate([vt_ref[vrows, k0:k0 + kn], jnp.ones((ONES_ROWS, kn), BF16)], axis=0)
        cm = jnp.max(st, axis=0, keepdims=True)
        if m is None:
            m = cm
            acc = _dot(vt, jnp.exp2(st - m).astype(BF16))
        else:
            m_new = jnp.maximum(m, cm)
            acc = acc * jnp.exp2(m - m_new) + _dot(vt, jnp.exp2(st - m_new).astype(BF16))
            m = m_new
    if bound is not None:
        inv = 1.0 / jnp.sum(psum, axis=0, keepdims=True)
    else:
        inv = 1.0 / acc[2 * MLA_V:2 * MLA_V + 1]
    o_t = jnp.concatenate([acc[0:MLA_V, 0:tq] * inv[:, 0:tq],
                           acc[MLA_V:2 * MLA_V, tq:] * inv[:, tq:]], axis=0)
    o_ref[:, vrows] = o_t.T.astype(o_ref.dtype)


def _attn_kernel(q_ref, k_ref, v_ref, o_ref, vt_ref, kmax_ref, *, lc):
    r = k_ref.shape[1]
    tq = q_ref.shape[1]
    i = pl.program_id(1)
    pairs = range(MLA_HEADS // 2)

    @pl.when(i == 0)
    def _():
        vt_ref[...] = v_ref[...].T
        sel = jnp.where((lax.broadcasted_iota(jnp.int32, (2 * HEAD_LANES, 2 * tq), 0) < HEAD_LANES)
                        == (lax.broadcasted_iota(jnp.int32, (2 * HEAD_LANES, 2 * tq), 1) < tq),
                        1.0, 0.0).astype(BF16)
        for hp in pairs:
            kmax = None
            for r0 in range(0, r, NORM_ROWS):
                k = k_ref[hp, r0:min(r0 + NORM_ROWS, r), :]
                cur = jnp.max(_dot(k * k, sel), axis=0, keepdims=True)
                kmax = cur if kmax is None else jnp.maximum(kmax, cur)
            kmax_ref[hp] = jnp.broadcast_to(kmax, kmax_ref.shape[1:])

    if lc:
        @pl.when(i == 0)
        def _():
            for hp in pairs:
                _attn_tile(q_ref, k_ref, vt_ref, o_ref, hp, lc)

    bounds = [_score_bound(_pair_queries(q_ref[hp]), kmax_ref[hp][0:1]) for hp in pairs]
    worst = bounds[0]
    for bnd in bounds[1:]:
        worst = jnp.maximum(worst, bnd)
    safe = jnp.max(worst) <= MAX_SAFE_BOUND
    latent = (i > 0) if lc else (i >= 0)

    @pl.when(jnp.logical_and(latent, safe))
    def _():
        for hp in pairs:
            _attn_tile(q_ref, k_ref, vt_ref, o_ref, hp, r, bounds[hp])

    @pl.when(jnp.logical_and(latent, jnp.logical_not(safe)))
    def _():
        for hp in pairs:
            _attn_tile(q_ref, k_ref, vt_ref, o_ref, hp, r)


def _attention(q, k, v, *, n_batch, lc, l):
    n_pairs, t_all, _ = q.shape
    r = lc + l
    tq = lc
    tiles = r // tq
    kern = functools.partial(_attn_kernel, lc=lc)
    return pl.pallas_call(
        kern,
        grid=(n_batch, tiles),
        in_specs=[
            pl.BlockSpec((n_pairs, tq, 2 * HEAD_LANES), lambda b, i: (0, b * tiles + i, 0)),
            pl.BlockSpec((n_pairs, r, 2 * HEAD_LANES), lambda b, i: (0, b, 0)),
            pl.BlockSpec((r, MLA_HEADS * MLA_V), lambda b, i: (b, 0)),
        ],
        out_specs=pl.BlockSpec((tq, MLA_HEADS * MLA_V), lambda b, i: (b * tiles + i, 0)),
        out_shape=jax.ShapeDtypeStruct((t_all, MLA_HEADS * MLA_V), BF16),
        scratch_shapes=[pltpu.VMEM((MLA_HEADS * MLA_V, r), BF16),
                        pltpu.VMEM((n_pairs, 8, 2 * tq), F32)],
        compiler_params=_cparams(("arbitrary", "arbitrary")),
        name="mla_attention",
    )(q, k, v)


def _pool_kernel(p_ref, w_ref, s_ref, o_ref, *, lc, l):
    r = lc + l
    t = lax.broadcasted_iota(jnp.int32, (r, POOL_GDIM), 0)
    in_lat = t >= lc
    pos = jnp.where(in_lat, t - lc, t)
    seg_len = jnp.where(in_lat, l, lc)

    def shifted(x, dlt):
        src = pos + dlt
        rolled = pltpu.roll(x, (-dlt) % r, axis=0)
        return jnp.where(src >= 0, jnp.where(src < seg_len, rolled, 0.0), 0.0)

    for g, w in enumerate(POOL_WINDOWS):
        half = w // 2
        cols = slice(g * POOL_GDIM, (g + 1) * POOL_GDIM)
        x = p_ref[:, cols]
        right = x
        left = x
        step = 1
        while step < half:
            right = right + shifted(right, step)
            left = left + shifted(left, -step)
            step *= 2
        win = shifted(left, -1) + right
        cnt = (jnp.minimum(pos + half, seg_len) - jnp.maximum(pos - half, 0)).astype(F32)
        pooled = (win / cnt - x).astype(BF16)
        o_ref[:, cols] = (_dot(pooled, w_ref[g]) * s_ref[:, cols]).astype(o_ref.dtype)


def _pool(pz, w_pool, pool_scale, *, n_batch, lc, l):
    t_all = pz.shape[0]
    r = lc + l
    kern = functools.partial(_pool_kernel, lc=lc, l=l)
    return pl.pallas_call(
        kern,
        grid=(n_batch,),
        in_specs=[pl.BlockSpec((r, BRANCH_W), lambda b: (b, 0)),
                  _const_spec(w_pool.shape), _const_spec((1, BRANCH_W))],
        out_specs=pl.BlockSpec((r, BRANCH_W), lambda b: (b, 0)),
        out_shape=jax.ShapeDtypeStruct((t_all, BRANCH_W), BF16),
        compiler_params=_cparams(("arbitrary",)),
        name="multiscale_pool",
    )(pz, w_pool, pool_scale)


def _gla_kernel(q_ref, k_ref, v_ref, ga_ref, gr_ref, wa_ref, ba_ref, go_ref, o_ref,
                qt_s, kts_s, ket_s, dec_s, vbd_s, og_s, *, lc, l, blk):
    r = lc + l
    c = GLA_CHUNK
    n_chunks = r // c
    n_ctx = lc // c
    cpb = blk // c
    w = 2 * GLA_DK
    bi_ = lax.broadcasted_iota(jnp.int32, (blk, blk), 0)
    bj_ = lax.broadcasted_iota(jnp.int32, (blk, blk), 1)
    same_chunk = (bi_ // c) == (bj_ // c)
    ones_m = jnp.where(same_chunk, 1.0, 0.0).astype(BF16)
    ind = jnp.where(lax.broadcasted_iota(jnp.int32, (blk, cpb * w), 0) // c
                    == lax.broadcasted_iota(jnp.int32, (blk, cpb * w), 1) // w, 1.0, 0.0).astype(BF16)
    head0_k = lax.broadcasted_iota(jnp.int32, (blk, w), 1) < GLA_DK
    head0_v = lax.broadcasted_iota(jnp.int32, (blk, 2 * GLA_DV), 1) < GLA_DV
    ci_ = lax.broadcasted_iota(jnp.int32, (c, w), 0)
    cj_ = lax.broadcasted_iota(jnp.int32, (c, w), 1) % c
    state_diag = ((lax.broadcasted_iota(jnp.int32, (w, 2 * GLA_DV), 0) < GLA_DK)
                  == (lax.broadcasted_iota(jnp.int32, (w, 2 * GLA_DV), 1) < GLA_DV))

    def split_store(dst, b, x0, x1):
        for cc in range(cpb):
            base = (b * cpb + cc) * 2 * c
            dst[pl.ds(pl.multiple_of(base, c), c), :] = x0[cc * c:(cc + 1) * c]
            dst[pl.ds(pl.multiple_of(base + c, c), c), :] = x1[cc * c:(cc + 1) * c]

    def vprep(b, carry):
        v = v_ref[pl.ds(pl.multiple_of(b * blk, blk), blk), :]
        zero = jnp.zeros_like(v)
        split_store(vbd_s, b, jnp.where(head0_v, v, zero), jnp.where(head0_v, zero, v))
        return carry

    lax.fori_loop(0, r // blk, vprep, 0)
    og_s[...] = jnp.zeros_like(og_s)

    dirs = (0, 1)
    tri_m = [jnp.where(same_chunk, jnp.where((bj_ <= bi_) if d == 0 else (bj_ >= bi_), 1.0, 0.0),
                       0.0).astype(BF16) for d in dirs]
    causal = [(cj_ <= ci_) if d == 0 else (cj_ >= ci_) for d in dirs]

    def prep(blocks):
        block_of = dict(enumerate(blocks))
        blocks = list(block_of)
        jobs = [(b, d) for b in blocks for d in dirs]
        rows = {b: pl.ds(pl.multiple_of(block_of[b] * blk, blk), blk) for b in blocks}
        ga = {b: ga_ref[rows[b], :] for b in blocks}
        logit = {j: _dot(ga[j[0]], wa_ref[j[1]]) + ba_ref[j[1]] for j in jobs}
        la = {j: (jnp.minimum(x, 0.0) - jnp.log(1.0 + jnp.exp(-jnp.abs(x)))) * (1.0 / GLA_TAU)
              for j, x in logit.items()}
        hl = {}
        for j in jobs:
            hi = la[j].astype(BF16)
            hl[j] = jnp.concatenate([hi, (la[j] - hi.astype(F32)).astype(BF16)], axis=1)
        cs = {j: _dot(tri_m[j[1]], hl[j]) for j in jobs}
        ts = {j: _dot(ones_m, hl[j]) for j in jobs}
        totb = {j: _dot_tn(hl[j], ind) for j in jobs}
        q = {b: q_ref[rows[b], :] * (GLA_DK ** -0.5) for b in blocks}
        k = {b: k_ref[rows[b], :] for b in blocks}
        for j in jobs:
            b, d = j
            first_chunk = block_of[b] * cpb
            bcum = cs[j][:, :w] + cs[j][:, w:]
            tot = ts[j][:, :w] + ts[j][:, w:]
            for cc in range(cpb):
                dec_s[d, first_chunk + cc] = jnp.exp(totb[j][:w, cc * w:(cc + 1) * w]
                                                 + totb[j][w:, cc * w:(cc + 1) * w])
            qt_s[d, rows[b], :] = (q[b] * jnp.exp(bcum)).astype(BF16)
            ke = (k[b] * jnp.exp(tot - bcum)).astype(BF16)
            kt = (k[b] * jnp.exp(-bcum)).astype(BF16)
            zero = jnp.zeros_like(kt)
            kt0, kt1 = jnp.where(head0_k, kt, zero), jnp.where(head0_k, zero, kt)
            for cc in range(cpb):
                crow = slice(cc * c, (cc + 1) * c)
                dst_rows = pl.ds(pl.multiple_of((first_chunk + cc) * 2 * c, 2 * c), 2 * c)
                kts_s[d, dst_rows, :] = jnp.concatenate([kt0[crow], kt1[crow]], axis=0).T
                ket_s[d, dst_rows, :] = ke[crow].T

    n_blocks = r // blk

    def prep_pair(t, carry):
        prep([2 * t, 2 * t + 1])
        return carry

    lax.fori_loop(0, n_blocks // 2, prep_pair, 0)
    if n_blocks % 2:
        prep([n_blocks - 1])

    def scan(t, sts):
        work = []
        for u in range(GLA_UNROLL):
            i = t * GLA_UNROLL + u
            for d in dirs:
                if d == 0:
                    n = i
                else:
                    n = jnp.where(i < n_ctx, n_ctx - 1 - i, n_chunks - 1 - (i - n_ctx))
                rows = pl.ds(pl.multiple_of(n * c, c), c)
                rows2 = pl.ds(pl.multiple_of(n * 2 * c, 2 * c), 2 * c)
                qc = qt_s[d, rows, :]
                a = jnp.where(causal[d], _dot(qc, kts_s[d, rows2, :]), 0.0).astype(BF16)
                dst = jnp.where(state_diag, _dot(ket_s[d, rows2, :], v_ref[rows, :]), 0.0)
                work.append((d, n, rows, rows2, qc, a, dst))
        sts = list(sts)
        for d, n, rows, rows2, qc, a, dst in work:
            o = _dot(jnp.concatenate([a, qc], axis=1),
                     jnp.concatenate([vbd_s[rows2, :], sts[d].astype(BF16)], axis=0))
            og_s[rows, :] = og_s[rows, :] + o
            dec = dec_s[d, n]
            sts[d] = sts[d] * jnp.concatenate([dec, dec], axis=1) + dst
        return tuple(sts)

    zero_state = jnp.zeros((w, 2 * GLA_DV), F32)
    lax.fori_loop(0, n_chunks // GLA_UNROLL, scan, (zero_state, zero_state))

    for hh in range(2):
        vc = slice(hh * GLA_DV, (hh + 1) * GLA_DV)
        on = _rms(og_s[:, vc], go_ref[...])
        o_ref[:, vc] = (on * _silu(gr_ref[:, vc])).astype(o_ref.dtype)


def _gla(gq, gk, gv, ga, gr, wa, ba, g_o, *, n_batch, lc, l):
    t_all = gq.shape[0]
    r = lc + l
    blk = 256 if r % 256 == 0 else 128
    kern = functools.partial(_gla_kernel, lc=lc, l=l, blk=blk)
    seq = lambda w: pl.BlockSpec((r, w), lambda b, hp: (b, hp), pipeline_mode=pl.Buffered(1))
    return pl.pallas_call(
        kern,
        grid=(n_batch, GLA_HEADS // 2),
        in_specs=[seq(2 * GLA_DK), seq(2 * GLA_DK), seq(2 * GLA_DV),
                  pl.BlockSpec((r, 128), lambda b, hp: (b, 0), pipeline_mode=pl.Buffered(1)),
                  seq(2 * GLA_DV),
                  pl.BlockSpec((2, 128, 2 * GLA_DK), lambda b, hp: (0, 0, hp)),
                  pl.BlockSpec((2, 1, 2 * GLA_DK), lambda b, hp: (0, 0, hp)),
                  pl.BlockSpec((1, GLA_DV), lambda b, hp: (0, 0))],
        out_specs=pl.BlockSpec((r, 2 * GLA_DV), lambda b, hp: (b, hp)),
        out_shape=jax.ShapeDtypeStruct((t_all, GLA_HEADS * GLA_DV), BF16),
        scratch_shapes=[pltpu.VMEM((2, r, 2 * GLA_DK), BF16), pltpu.VMEM((2, 2 * r, 2 * GLA_DK), BF16),
                        pltpu.VMEM((2, 2 * r, GLA_CHUNK), BF16),
                        pltpu.VMEM((2, r // GLA_CHUNK, 2 * GLA_DK, 2 * GLA_DK), F32),
                        pltpu.VMEM((2 * r, 2 * GLA_DV), BF16), pltpu.VMEM((r, 2 * GLA_DV), F32)],
        compiler_params=_cparams(("arbitrary", "arbitrary")),
        name="gla_bidir",
    )(gq, gk, gv, ga, gr, wa, ba, g_o)


def _merge_kernel(h_ref, modb_ref, modc_ref, g_ref, oa_ref, op_ref, og_ref, wg_ref, wb_ref,
                  wo_ref, o_ref, *, tm, lc, tiles_per_batch):
    d = h_ref.shape[-1]
    j = pl.program_id(0) % tiles_per_batch
    shift, scale, gate = _mod_rows(modb_ref, modc_ref, j, tm, d, lc)
    h = h_ref[...]
    u = (_rms(h, g_ref[...]) * (1.0 + scale) + shift).astype(BF16)
    m = None
    for br, src in enumerate((oa_ref, op_ref, og_ref)):
        gt = _sigmoid(_dot(u, wg_ref[:, br * d:(br + 1) * d]))
        term = gt * _dot(src[...], wb_ref[br])
        m = term if m is None else m + term
    y = _dot(m.astype(BF16), wo_ref[...])
    o_ref[...] = h + gate * y


def _merge(h, mod, layer, g_mix, o_a, o_p, o_g, w_gates, w_branch, w_out, *, n_batch, lc, l):
    t_all, d = h.shape
    r = lc + l
    tm = r // 8 if (r % 128 == 0 and r // 8 >= 128) else r // 2
    tiles_per_batch = r // tm
    kern = functools.partial(_merge_kernel, tm=tm, lc=lc, tiles_per_batch=tiles_per_batch)
    tok = lambda w: pl.BlockSpec((tm, w), lambda i: (i, 0))
    return pl.pallas_call(
        kern,
        grid=(t_all // tm,),
        in_specs=[tok(d)] + _mod_specs(mod, layer, 1, tiles_per_batch, n_batch) + [
            _const_spec((1, d)), tok(BRANCH_W), tok(BRANCH_W), tok(BRANCH_W),
            _const_spec(w_gates.shape), _const_spec(w_branch.shape), _const_spec(w_out.shape)],
        out_specs=tok(d),
        out_shape=jax.ShapeDtypeStruct((t_all, d), F32),
        compiler_params=_cparams(("arbitrary",)),
        name="branch_merge",
    )(h, mod, mod, g_mix, o_a, o_p, o_g, w_gates, w_branch, w_out)


_ROPE_PARTNER = np.arange(MLA_ROPE) ^ (AXIS_DIM // 2)


def _rope_table(lc, l):
    pos = jnp.arange(l)
    row = (pos // GRID_W).astype(F32)
    col = (pos % GRID_W).astype(F32)
    inv = ROPE_THETA ** (-jnp.arange(0, AXIS_DIM, 2, dtype=F32) / AXIS_DIM)
    ang = jnp.stack([row, col], axis=-1)[:, :, None] * inv
    cos = jnp.repeat(jnp.cos(ang)[:, :, None, :], 2, axis=2).reshape(l, MLA_ROPE)
    sin = jnp.repeat(jnp.sin(ang)[:, :, None, :], 2, axis=2)
    sin = (sin * jnp.array([-1.0, 1.0], F32)[None, None, :, None]).reshape(l, MLA_ROPE)
    pad = HEAD_LANES - MLA_QK
    tab_c = jnp.concatenate([
        jnp.concatenate([jnp.ones((lc, MLA_QK), F32), jnp.zeros((lc, pad), F32)], axis=1),
        jnp.concatenate([jnp.ones((l, MLA_NOPE), F32), cos, jnp.zeros((l, pad), F32)], axis=1)],
        axis=0)
    tab_s = jnp.concatenate([
        jnp.zeros((lc, HEAD_LANES), F32),
        jnp.concatenate([jnp.zeros((l, MLA_NOPE), F32), sin, jnp.zeros((l, pad), F32)], axis=1)],
        axis=0)
    return tab_c, tab_s


def _pack_layer(p, layer):
    d = p["w_in"].shape[1]
    w_in = p["w_in"][layer]
    offs = np.cumsum([0, MLA_Q_RANK, MLA_KV_RANK, MLA_ROPE, BRANCH_W, GLA_HEADS * GLA_DK,
                      GLA_HEADS * GLA_DK, GLA_HEADS * GLA_DV, 2 * GLA_GATE_RANK,
                      GLA_HEADS * GLA_DV, N_BRANCH * d])
    cq, ckv, kr, pz, gq, gk, gv, ga, gr, gates = [w_in[:, offs[i]:offs[i + 1]] for i in range(10)]
    pad = HEAD_LANES - MLA_QK

    def rope_lanes(x):
        lead = x.shape[:-1]
        return jnp.concatenate([jnp.zeros(lead + (MLA_NOPE,), F32), x, jnp.zeros(lead + (pad,), F32)],
                               axis=-1)

    ga_ext = jnp.concatenate([ga, jnp.zeros((d, 128 - 2 * GLA_GATE_RANK), F32)], axis=1)
    w_all = jnp.concatenate([cq, ckv, rope_lanes(kr), rope_lanes(kr[:, _ROPE_PARTNER]),
                             pz, gq, gk, gv, ga_ext, gr], axis=1).astype(BF16)

    w_uq = p["w_uq"][layer].reshape(MLA_Q_RANK, MLA_HEADS, MLA_QK)
    w_uq_a = jnp.concatenate([w_uq, jnp.zeros((MLA_Q_RANK, MLA_HEADS, pad), F32)], axis=2)
    w_uq_b = rope_lanes(w_uq[:, :, MLA_NOPE + _ROPE_PARTNER])
    w_uq = jnp.concatenate([w_uq_a.reshape(MLA_Q_RANK, -1), w_uq_b.reshape(MLA_Q_RANK, -1)],
                           axis=1).astype(BF16)
    w_ukv = p["w_ukv"][layer].reshape(MLA_KV_RANK, MLA_HEADS, MLA_NOPE + MLA_V)
    w_uk = jnp.concatenate([w_ukv[:, :, :MLA_NOPE],
                            jnp.zeros((MLA_KV_RANK, MLA_HEADS, HEAD_LANES - MLA_NOPE), F32)], axis=2)
    w_uk = w_uk.reshape(MLA_KV_RANK, MLA_HEADS * HEAD_LANES).astype(BF16)
    w_uv = w_ukv[:, :, MLA_NOPE:].reshape(MLA_KV_RANK, MLA_HEADS * MLA_V).astype(BF16)

    def gain_tab(g):
        return jnp.stack([jnp.concatenate([g, jnp.zeros((pad,), F32)]),
                          rope_lanes(g[MLA_NOPE + _ROPE_PARTNER])])

    wa = jnp.zeros((2, 128, GLA_HEADS * GLA_DK), F32)
    for dr in range(2):
        wa = wa.at[dr, dr * GLA_GATE_RANK:(dr + 1) * GLA_GATE_RANK].set(p["w_a2"][layer, dr])

    return dict(
        g_mix=p["g_mix"][layer][None, :], w_all=w_all,
        g_cq=p["g_cq"][layer][None, :], w_uq=w_uq,
        g_ckv=p["g_ckv"][layer][None, :], w_uk=w_uk, w_uv=w_uv,
        gq_tab=gain_tab(p["g_qn"][layer]) * (MLA_QK ** -0.5 * math.log2(math.e)), gk_tab=gain_tab(p["g_kn"][layer]),
        w_gates=gates.astype(BF16),
        w_pool=p["w_pool"][layer].astype(BF16), pool_scale=p["pool_scale"][layer][None, :],
        wa=wa.astype(BF16), ba=p["b_a2"][layer][:, None, :], g_gla_o=p["g_gla_o"][layer][None, :],
        w_branch=p["w_branch"][layer].astype(BF16), w_out=p["w_out"][layer].astype(BF16),
    )


def kernel(x, c, ctx, c_ctx, w_ada, b_ada, g_ffn1, ffn1_w1, ffn1_w3, ffn1_w2, g_mix, w_in,
           g_cq, w_uq, g_ckv, w_ukv, g_qn, g_kn, w_pool, pool_scale, w_a2, b_a2, g_gla_o,
           w_branch, w_out, g_ffn2, ffn2_w1, ffn2_w3, ffn2_w2):
    n_batch, l, d = x.shape
    lc = ctx.shape[1]
    depth = w_ada.shape[0]
    dims = dict(n_batch=n_batch, lc=lc, l=l)
    p = dict(w_in=w_in, g_mix=g_mix, g_cq=g_cq, w_uq=w_uq, g_ckv=g_ckv, w_ukv=w_ukv, g_qn=g_qn,
             g_kn=g_kn, w_pool=w_pool, pool_scale=pool_scale, w_a2=w_a2, b_a2=b_a2,
             g_gla_o=g_gla_o, w_branch=w_branch, w_out=w_out)

    cond_rows = 16
    cond = jnp.zeros((cond_rows, d), F32).at[:n_batch].set(c).at[n_batch].set(c_ctx)
    mod = _ada_mod(cond, w_ada, b_ada).reshape(depth, cond_rows, 3, 3, d)

    tab = _rope_table(lc, l)
    f1_w1, f1_w3, f2_w1, f2_w3 = _cast_bf16(ffn1_w1, ffn1_w3, ffn2_w1, ffn2_w3)
    f1_w2, f2_w2 = _cast_bf16(ffn1_w2, ffn2_w2)
    h = x.reshape(n_batch * l, d)
    for layer in range(depth):
        last = layer == depth - 1
        pw = _pack_layer(p, layer)
        first = dict(ctx_src=ctx.reshape(n_batch * lc, d)) if layer == 0 else {}
        h = _ffn(h, mod, layer, 0, g_ffn1[layer], f1_w1, f1_w3, f1_w2, **first, **dims)
        q, k, v, pz, gq, gk, gv, ga, gr = _mixin(h, mod, layer, pw, tab, **dims)
        o_a = _attention(q, k, v, **dims)
        o_p = _pool(pz, pw["w_pool"], pw["pool_scale"], **dims)
        o_g = _gla(gq, gk, gv, ga, gr, pw["wa"], pw["ba"], pw["g_gla_o"], **dims)
        h = _merge(h, mod, layer, pw["g_mix"], o_a, o_p, o_g, pw["w_gates"], pw["w_branch"],
                   pw["w_out"], **dims)
        h = _ffn(h, mod, layer, 2, g_ffn2[layer], f2_w1, f2_w3, f2_w2, lat_only=last, **dims)
    return h.reshape(n_batch, l, d)
```

```python
import functools
import math

import jax
import jax.numpy as jnp
import numpy as np
from jax import lax
from jax.experimental import pallas as pl
from jax.experimental.pallas import tpu as pltpu

F32 = jnp.float32
BF16 = jnp.bfloat16

EPS = 1e-6
GRID_W = 64
N_MOD = 9
MLA_HEADS = 8
MLA_Q_RANK = 256
MLA_KV_RANK = 128
MLA_NOPE = 64
MLA_ROPE = 32
MLA_V = 64
MLA_QK = MLA_NOPE + MLA_ROPE
AXIS_DIM = MLA_ROPE // 2
ROPE_THETA = 10000.0
POOL_WINDOWS = (2, 4, 8, 16)
BRANCH_W = 512
POOL_GDIM = BRANCH_W // len(POOL_WINDOWS)
GLA_HEADS = 4
GLA_DK = 64
GLA_DV = 128
GLA_GATE_RANK = 16
GLA_TAU = 16.0
GLA_CHUNK = 64
GLA_UNROLL = 4
N_BRANCH = 3
HEAD_LANES = 128

C_CQ, C_CKV, C_KRA, C_KRB, C_PZ, C_GQ, C_GK, C_GV, C_GA, C_GR, C_END = (
    0, 256, 384, 512, 640, 1152, 1408, 1664, 2176, 2304, 2816)

VMEM_LIMIT = 56 * 1024 * 1024
MXU_ROW_GROUP = 64
CAST_ROWS = 256
MERGE_COLS = 512


def _cparams(sem):
    return pltpu.CompilerParams(dimension_semantics=sem, vmem_limit_bytes=VMEM_LIMIT)


def _const_spec(shape):
    nd = len(shape)
    return pl.BlockSpec(shape, lambda *_: (0,) * nd, pipeline_mode=pl.Buffered(1))


def _dot(a, b):
    return jnp.dot(a, b, preferred_element_type=F32)


def _dot_nt(a, b):
    return lax.dot_general(a, b, (((1,), (1,)), ((), ())), preferred_element_type=F32)


def _dot_tn(a, b):
    return lax.dot_general(a, b, (((0,), (0,)), ((), ())), preferred_element_type=F32)


def _silu(x):
    return x * (1.0 / (1.0 + jnp.exp(-x)))


def _sigmoid(x):
    return 1.0 / (1.0 + jnp.exp(-x))


def _rms(x, g):
    ms = jnp.mean(x * x, axis=-1, keepdims=True)
    return x * lax.rsqrt(ms + EPS) * g


def _ada_kernel(c_ref, w_ref, b_ref, o_ref):
    s = _silu(c_ref[...]).astype(BF16)
    o_ref[...] = _dot(s, w_ref[...].astype(BF16)) + b_ref[...]


def _ada_mod(cond, w_ada, b_ada):
    depth, d, n = w_ada.shape
    rows = cond.shape[0]
    tn = 1024
    return pl.pallas_call(
        _ada_kernel,
        grid=(depth, n // tn),
        in_specs=[
            pl.BlockSpec((rows, d), lambda l, j: (0, 0)),
            pl.BlockSpec((None, d, tn), lambda l, j: (l, 0, j)),
            pl.BlockSpec((None, 1, tn), lambda l, j: (l, 0, j)),
        ],
        out_specs=pl.BlockSpec((None, rows, tn), lambda l, j: (l, 0, j)),
        out_shape=jax.ShapeDtypeStruct((depth, rows, n), F32),
        compiler_params=_cparams(("arbitrary", "arbitrary")),
        name="ada_mod",
    )(cond, w_ada, b_ada.reshape(depth, 1, n))


def _cast_kernel(*refs):
    n = len(refs) // 2
    for src, dst in zip(refs[:n], refs[n:]):
        dst[...] = src[...].astype(BF16)


def _cast_bf16(*ws):
    depth, rows, cols = ws[0].shape
    tr = CAST_ROWS if rows % CAST_ROWS == 0 else rows
    spec = pl.BlockSpec((None, tr, cols), lambda l, i: (l, i, 0))
    return pl.pallas_call(
        _cast_kernel,
        grid=(depth, rows // tr),
        in_specs=[spec] * len(ws),
        out_specs=[spec] * len(ws),
        out_shape=[jax.ShapeDtypeStruct(w.shape, BF16) for w in ws],
        compiler_params=_cparams(("arbitrary", "arbitrary")),
        name="cast_weights",
    )(*ws)


def _mod_rows(modb_ref, modc_ref, tile_in_batch, tm, d, lc):
    modb = modb_ref[...]
    modc = modc_ref[...]
    if lc == 0:
        return modb[0:1], modb[1:2], modb[2:3]
    row = tile_in_batch * tm + lax.broadcasted_iota(jnp.int32, (tm, d), 0)
    is_ctx = row < lc
    return tuple(jnp.where(is_ctx, modc[i:i + 1], modb[i:i + 1]) for i in range(3))


def _mod_specs(mod, layer, group, tiles_per_batch, n_batch):
    d = mod.shape[-1]
    blk = (None, None, None, 3, d)
    return [
        pl.BlockSpec(blk, lambda i: (layer, i // tiles_per_batch, group, 0, 0)),
        pl.BlockSpec(blk, lambda i: (layer, n_batch, group, 0, 0)),
    ]


def _ffn_kernel(h_ref, c_ref, modb_ref, modc_ref, g_ref, w1_ref, w3_ref, w2_ref, o_ref, hid_ref,
                *, tm, lc, tiles_per_batch, col_chunk, ctx_tiles):
    d = h_ref.shape[-1]
    j = pl.program_id(0) % tiles_per_batch
    shift, scale, gate = _mod_rows(modb_ref, modc_ref, j, tm, d, lc)

    def read(rows):
        if ctx_tiles:
            return jnp.where(j < ctx_tiles, c_ref[rows, :], h_ref[rows, :])
        return h_ref[rows, :]
    if tm >= 8 * MXU_ROW_GROUP and tm % MXU_ROW_GROUP == 0:
        cut = (tm // MXU_ROW_GROUP + 1) // 2 * MXU_ROW_GROUP
        groups = [slice(0, cut), slice(cut, tm)]
    else:
        groups = [slice(0, tm)]

    def rows_of(x, rows):
        return x if x.shape[0] == 1 else x[rows]

    u = [(_rms(read(rows), g_ref[...]) * (1.0 + rows_of(scale, rows))
          + rows_of(shift, rows)).astype(BF16) for rows in groups]
    d_ff = w1_ref.shape[-1]
    for c0 in range(0, d_ff, col_chunk):
        cols = slice(c0, min(c0 + col_chunk, d_ff))
        for rows, ug in zip(groups, u):
            a = _dot(ug, w1_ref[:, cols])
            b = _dot(ug, w3_ref[:, cols])
            hid_ref[rows, cols] = (_silu(a) * b).astype(BF16)
    for rows in groups:
        y = _dot(hid_ref[rows, :], w2_ref[...])
        o_ref[rows, :] = read(rows) + 0.5 * rows_of(gate, rows) * y


def _ffn(h, mod, layer, group, g, w1, w3, w2, *, n_batch, lc, l, lat_only=False, ctx_src=None):
    d = h.shape[-1]
    t_all = n_batch * (lc + l)
    d_ff = w1.shape[-1]
    r = lc + l
    ctx_tiles = 0
    ctx_spec = pl.BlockSpec((8, d), lambda i: (0, 0))
    ctx_arr = h
    if ctx_src is not None:
        tm = min(math.gcd(lc, l), 256)
        tiles_per_batch = r // tm
        ctx_tiles = lc // tm
        lat_tiles = l // tm
        rows_out = t_all
        in_map = lambda i: ((i // tiles_per_batch) * lat_tiles
                            + jnp.maximum(i % tiles_per_batch - ctx_tiles, 0), 0)
        ctx_spec = pl.BlockSpec((tm, d), lambda i: ((i // tiles_per_batch) * ctx_tiles
                                                    + jnp.minimum(i % tiles_per_batch, ctx_tiles - 1), 0))
        ctx_arr = ctx_src
        lc_k = lc
    elif lat_only:
        tm = math.gcd(lc, l) if lc else l
        tm = min(tm, 256)
        tiles_per_batch = l // tm
        skip = lc // tm
        rows_out = n_batch * l
        in_map = lambda i: ((i // tiles_per_batch) * (r // tm) + skip + i % tiles_per_batch, 0)
        lc_k = 0
    else:
        tm = r // 4
        tiles_per_batch = r // tm
        rows_out = t_all
        in_map = lambda i: (i, 0)
        lc_k = lc
    col_chunk = min(d_ff, 512)
    kern = functools.partial(_ffn_kernel, tm=tm, lc=lc_k, tiles_per_batch=tiles_per_batch,
                             col_chunk=col_chunk, ctx_tiles=ctx_tiles)
    return pl.pallas_call(
        kern,
        grid=(rows_out // tm,),
        in_specs=[pl.BlockSpec((tm, d), in_map), ctx_spec]
        + _mod_specs(mod, layer, group, tiles_per_batch, n_batch)
        + [_const_spec((1, d))]
        + [pl.BlockSpec((None,) + w.shape[1:], lambda i: (layer, 0, 0), pipeline_mode=pl.Buffered(1))
           for w in (w1, w3, w2)],
        out_specs=pl.BlockSpec((tm, d), lambda i: (i, 0)),
        out_shape=jax.ShapeDtypeStruct((rows_out, d), F32),
        scratch_shapes=[pltpu.VMEM((tm, d_ff), BF16)],
        compiler_params=_cparams(("arbitrary",)),
        name="ffn_half_step",
    )(h, ctx_arr, mod, mod, g.reshape(1, d), w1, w3, w2)


def _head_norm_rope(xa, rot, tab_a):
    ss = jnp.sum(xa * xa, axis=-1, keepdims=True) * (1.0 / MLA_QK)
    return ((xa * tab_a + rot) * lax.rsqrt(ss + EPS)).astype(BF16)


def _mixin_kernel(h_ref, modb_ref, modc_ref, g_ref, w_ref, gcq_ref, wuq_ref, gckv_ref,
                  wuk_ref, wuv_ref, gq_ref, gk_ref, tabc_ref, tabs_ref,
                  q_out, k_out, v_out, pz_out, gq_out, gk_out, gv_out, ga_out, gr_out,
                  *, tm, lc, tiles_per_batch):
    d = h_ref.shape[-1]
    j = pl.program_id(0) % tiles_per_batch
    shift, scale, _ = _mod_rows(modb_ref, modc_ref, j, tm, d, lc)
    u = (_rms(h_ref[...], g_ref[...]) * (1.0 + scale) + shift).astype(BF16)

    lowrank = _dot(u, w_ref[:, C_CQ:C_PZ])
    cqn = _rms(lowrank[:, C_CQ:C_CKV], gcq_ref[...]).astype(BF16)
    ckvn = _rms(lowrank[:, C_CKV:C_KRA], gckv_ref[...]).astype(BF16)
    kr_a = lowrank[:, C_KRA:C_KRB]
    kr_b = lowrank[:, C_KRB:C_PZ]
    q_pre = _dot(cqn, wuq_ref[...])
    k_pre = _dot(ckvn, wuk_ref[...])
    v_out[...] = _dot(ckvn, wuv_ref[...]).astype(BF16)

    tab_c = tabc_ref[...]
    tab_s = tabs_ref[...]
    gq = gq_ref[...]
    gk = gk_ref[...]
    qa_tab, qb_tab = tab_c * gq[0:1], tab_s * gq[1:2]
    ka_tab = tab_c * gk[0:1]
    k_rot = kr_b * (tab_s * gk[1:2])
    n_q = MLA_HEADS * HEAD_LANES

    def other(idx):
        if idx == 0:
            pz_out[...] = _dot(u, w_ref[:, C_PZ:C_GQ])
        elif idx == 1:
            gq_out[...] = _dot(u, w_ref[:, C_GQ:C_GK])
        elif idx == 2:
            gk_out[...] = _dot(u, w_ref[:, C_GK:C_GV])
        elif idx == 3:
            gv_out[...] = _dot(u, w_ref[:, C_GV:C_GA]).astype(BF16)
        elif idx == 4:
            ga_out[...] = _dot(u, w_ref[:, C_GA:C_GR]).astype(BF16)
        elif idx == 5:
            gr_out[...] = _dot(u, w_ref[:, C_GR:C_END])

    for hd in range(MLA_HEADS):
        cols = slice(hd * HEAD_LANES, (hd + 1) * HEAD_LANES)
        cols_b = slice(n_q + hd * HEAD_LANES, n_q + (hd + 1) * HEAD_LANES)
        pair, half = hd // 2, slice((hd % 2) * HEAD_LANES, (hd % 2 + 1) * HEAD_LANES)
        q_out[pair, :, half] = _head_norm_rope(q_pre[:, cols], q_pre[:, cols_b] * qb_tab, qa_tab)
        k_out[pair, :, half] = _head_norm_rope(k_pre[:, cols] + kr_a, k_rot, ka_tab)
        other(hd)


def _mixin(h, mod, layer, pw, tab, *, n_batch, lc, l):
    t_all, d = h.shape
    r = lc + l
    tm = r // 8 if (r % 128 == 0 and r // 8 >= 128) else r // 2
    tiles_per_batch = r // tm
    kern = functools.partial(_mixin_kernel, tm=tm, lc=lc, tiles_per_batch=tiles_per_batch)
    tok = lambda w: pl.BlockSpec((tm, w), lambda i: (i, 0))
    n_pairs = MLA_HEADS // 2
    qk_spec = pl.BlockSpec((n_pairs, tm, 2 * HEAD_LANES), lambda i: (0, i, 0))
    qk_shape = jax.ShapeDtypeStruct((n_pairs, t_all, 2 * HEAD_LANES), BF16)
    outs = [(MLA_HEADS * MLA_V, BF16), (BRANCH_W, F32), (GLA_HEADS * GLA_DK, F32),
            (GLA_HEADS * GLA_DK, F32), (GLA_HEADS * GLA_DV, BF16), (128, BF16),
            (GLA_HEADS * GLA_DV, F32)]
    return pl.pallas_call(
        kern,
        grid=(t_all // tm,),
        in_specs=[tok(d)] + _mod_specs(mod, layer, 1, tiles_per_batch, n_batch) + [
            _const_spec((1, d)), _const_spec((d, C_END)),
            _const_spec((1, MLA_Q_RANK)), _const_spec((MLA_Q_RANK, 2 * MLA_HEADS * HEAD_LANES)),
            _const_spec((1, MLA_KV_RANK)), _const_spec((MLA_KV_RANK, MLA_HEADS * HEAD_LANES)),
            _const_spec((MLA_KV_RANK, MLA_HEADS * MLA_V)),
            _const_spec((2, HEAD_LANES)), _const_spec((2, HEAD_LANES)),
            pl.BlockSpec((tm, HEAD_LANES), lambda i: (i % tiles_per_batch, 0)),
            pl.BlockSpec((tm, HEAD_LANES), lambda i: (i % tiles_per_batch, 0)),
        ],
        out_specs=[qk_spec, qk_spec] + [tok(w) for w, _ in outs],
        out_shape=[qk_shape, qk_shape] + [jax.ShapeDtypeStruct((t_all, w), dt) for w, dt in outs],
        compiler_params=_cparams(("arbitrary",)),
        name="mixer_in_proj",
    )(h, mod, mod, pw["g_mix"], pw["w_all"], pw["g_cq"], pw["w_uq"], pw["g_ckv"],
      pw["w_uk"], pw["w_uv"], pw["gq_tab"], pw["gk_tab"], tab[0], tab[1])


ONES_ROWS = 16
KEY_CHUNK = 256
SCORE_LOOKAHEAD = 5
BOUND_MARGIN = 1.02
MAX_SAFE_BOUND = 50.0
NORM_ROWS = 1088


def _pair_queries(q):
    qt = q.T
    row = lax.broadcasted_iota(jnp.int32, qt.shape, 0)
    zero = jnp.zeros_like(qt)
    return jnp.concatenate([jnp.where(row < HEAD_LANES, qt, zero),
                            jnp.where(row < HEAD_LANES, zero, qt)], axis=1)


def _score_bound(qbd, kmax_row):
    qf = qbd.astype(F32)
    qn2 = jnp.sum(qf * qf, axis=0, keepdims=True)
    return jnp.sqrt(qn2 * kmax_row) * BOUND_MARGIN


def _attn_tile(q_ref, k_ref, vt_ref, o_ref, hp, n_keys, bound=None):
    tq = q_ref.shape[1]
    vrows = slice(hp * 2 * MLA_V, (hp + 1) * 2 * MLA_V)
    qbd = _pair_queries(q_ref[hp])
    chunks = [(k0, min(KEY_CHUNK, n_keys - k0)) for k0 in range(0, n_keys, KEY_CHUNK)]

    def scores(k0, kn):
        return _dot(k_ref[hp, k0:k0 + kn, :], qbd)

    pending = [scores(*ch) for ch in chunks[:SCORE_LOOKAHEAD]]
    m = acc = psum = None
    for ci, (k0, kn) in enumerate(chunks):
        if ci + SCORE_LOOKAHEAD < len(chunks):
            pending.append(scores(*chunks[ci + SCORE_LOOKAHEAD]))
        st = pending.pop(0)
        if bound is not None:
            e = jnp.exp2(st - bound)
            part_sum = jnp.sum(e.reshape(kn // 8, 8, 2 * tq), axis=0)
            part = _dot(vt_ref[vrows, k0:k0 + kn], e.astype(BF16))
            acc = part if acc is None else acc + part
            psum = part_sum if psum is None else psum + part_sum
            continue
        vt = jnp.concatenate([vt_ref[vrows, k0:k0 + kn], jnp.ones((ONES_ROWS, kn), BF16)], axis=0)
        cm = jnp.max(st, axis=0, keepdims=True)
        if m is None:
            m = cm
            acc = _dot(vt, jnp.exp2(st - m).astype(BF16))
        else:
            m_new = jnp.maximum(m, cm)
            acc = acc * jnp.exp2(m - m_new) + _dot(vt, jnp.exp2(st - m_new).astype(BF16))
            m = m_new
    if bound is not None:
        inv = 1.0 / jnp.sum(psum, axis=0, keepdims=True)
    else:
        inv = 1.0 / acc[2 * MLA_V:2 * MLA_V + 1]
    o_t = jnp.concatenate([acc[0:MLA_V, 0:tq] * inv[:, 0:tq],
                           acc[MLA_V:2 * MLA_V, tq:] * inv[:, tq:]], axis=0)
    o_ref[:, vrows] = o_t.T.astype(o_ref.dtype)


def _attn_kernel(q_ref, k_ref, v_ref, o_ref, vt_ref, kmax_ref, *, lc):
    r = k_ref.shape[1]
    tq = q_ref.shape[1]
    i = pl.program_id(1)
    pairs = range(MLA_HEADS // 2)

    @pl.when(i == 0)
    def _():
        vt_ref[...] = v_ref[...].T
        pair_w = 2 * HEAD_LANES
        sel = jnp.where((lax.broadcasted_iota(jnp.int32, (pair_w, pair_w), 0) < HEAD_LANES)
                        == (lax.broadcasted_iota(jnp.int32, (pair_w, pair_w), 1) < HEAD_LANES),
                        1.0, 0.0).astype(BF16)
        for hp in pairs:
            kmax = None
            for r0 in range(0, r, NORM_ROWS):
                k = k_ref[hp, r0:min(r0 + NORM_ROWS, r), :]
                cur = jnp.max(_dot(k * k, sel), axis=0, keepdims=True)
                kmax = cur if kmax is None else jnp.maximum(kmax, cur)
            reps = tq // HEAD_LANES
            row = jnp.concatenate([kmax[:, :HEAD_LANES]] * reps + [kmax[:, HEAD_LANES:]] * reps,
                                  axis=1)
            kmax_ref[hp] = jnp.broadcast_to(row, kmax_ref.shape[1:])

    if lc:
        @pl.when(i == 0)
        def _():
            for hp in pairs:
                _attn_tile(q_ref, k_ref, vt_ref, o_ref, hp, lc)

    bounds = [_score_bound(_pair_queries(q_ref[hp]), kmax_ref[hp][0:1]) for hp in pairs]
    worst = bounds[0]
    for bnd in bounds[1:]:
        worst = jnp.maximum(worst, bnd)
    safe = jnp.max(worst) <= MAX_SAFE_BOUND
    latent = (i > 0) if lc else (i >= 0)

    @pl.when(jnp.logical_and(latent, safe))
    def _():
        for hp in pairs:
            _attn_tile(q_ref, k_ref, vt_ref, o_ref, hp, r, bounds[hp])

    @pl.when(jnp.logical_and(latent, jnp.logical_not(safe)))
    def _():
        for hp in pairs:
            _attn_tile(q_ref, k_ref, vt_ref, o_ref, hp, r)


def _attention(q, k, v, *, n_batch, lc, l):
    n_pairs, t_all, _ = q.shape
    r = lc + l
    tq = lc
    tiles = r // tq
    kern = functools.partial(_attn_kernel, lc=lc)
    return pl.pallas_call(
        kern,
        grid=(n_batch, tiles),
        in_specs=[
            pl.BlockSpec((n_pairs, tq, 2 * HEAD_LANES), lambda b, i: (0, b * tiles + i, 0)),
            pl.BlockSpec((n_pairs, r, 2 * HEAD_LANES), lambda b, i: (0, b, 0)),
            pl.BlockSpec((r, MLA_HEADS * MLA_V), lambda b, i: (b, 0)),
        ],
        out_specs=pl.BlockSpec((tq, MLA_HEADS * MLA_V), lambda b, i: (b * tiles + i, 0)),
        out_shape=jax.ShapeDtypeStruct((t_all, MLA_HEADS * MLA_V), BF16),
        scratch_shapes=[pltpu.VMEM((MLA_HEADS * MLA_V, r), BF16),
                        pltpu.VMEM((n_pairs, 8, 2 * tq), F32)],
        compiler_params=_cparams(("arbitrary", "arbitrary")),
        name="mla_attention",
    )(q, k, v)


def _pool_kernel(p_ref, w_ref, s_ref, o_ref, *, lc, l):
    r = lc + l
    t = lax.broadcasted_iota(jnp.int32, (r, POOL_GDIM), 0)
    in_lat = t >= lc
    pos = jnp.where(in_lat, t - lc, t)
    seg_len = jnp.where(in_lat, l, lc)

    def shifted(x, dlt):
        src = pos + dlt
        rolled = pltpu.roll(x, (-dlt) % r, axis=0)
        return jnp.where(src >= 0, jnp.where(src < seg_len, rolled, 0.0), 0.0)

    for g, w in enumerate(POOL_WINDOWS):
        half = w // 2
        cols = slice(g * POOL_GDIM, (g + 1) * POOL_GDIM)
        x = p_ref[:, cols]
        right = x
        left = x
        step = 1
        while step < half:
            right = right + shifted(right, step)
            left = left + shifted(left, -step)
            step *= 2
        win = shifted(left, -1) + right
        cnt = (jnp.minimum(pos + half, seg_len) - jnp.maximum(pos - half, 0)).astype(F32)
        pooled = (win / cnt - x).astype(BF16)
        o_ref[:, cols] = (_dot(pooled, w_ref[g]) * s_ref[:, cols]).astype(o_ref.dtype)


def _pool(pz, w_pool, pool_scale, *, n_batch, lc, l):
    t_all = pz.shape[0]
    r = lc + l
    kern = functools.partial(_pool_kernel, lc=lc, l=l)
    return pl.pallas_call(
        kern,
        grid=(n_batch,),
        in_specs=[pl.BlockSpec((r, BRANCH_W), lambda b: (b, 0)),
                  _const_spec(w_pool.shape), _const_spec((1, BRANCH_W))],
        out_specs=pl.BlockSpec((r, BRANCH_W), lambda b: (b, 0)),
        out_shape=jax.ShapeDtypeStruct((t_all, BRANCH_W), BF16),
        compiler_params=_cparams(("arbitrary",)),
        name="multiscale_pool",
    )(pz, w_pool, pool_scale)


def _gla_kernel(q_ref, k_ref, v_ref, ga_ref, gr_ref, wa_ref, ba_ref, go_ref, o_ref,
                qt_s, kts_s, ket_s, dec_s, vbd_s, og_s, *, lc, l, blk):
    r = lc + l
    c = GLA_CHUNK
    n_chunks = r // c
    n_ctx = lc // c
    cpb = blk // c
    w = 2 * GLA_DK
    bi_ = lax.broadcasted_iota(jnp.int32, (blk, blk), 0)
    bj_ = lax.broadcasted_iota(jnp.int32, (blk, blk), 1)
    same_chunk = (bi_ // c) == (bj_ // c)
    ones_m = jnp.where(same_chunk, 1.0, 0.0).astype(BF16)
    ind = jnp.where(lax.broadcasted_iota(jnp.int32, (blk, cpb * w), 0) // c
                    == lax.broadcasted_iota(jnp.int32, (blk, cpb * w), 1) // w, 1.0, 0.0).astype(BF16)
    head0_k = lax.broadcasted_iota(jnp.int32, (blk, w), 1) < GLA_DK
    head0_v = lax.broadcasted_iota(jnp.int32, (blk, 2 * GLA_DV), 1) < GLA_DV
    ci_ = lax.broadcasted_iota(jnp.int32, (c, w), 0)
    cj_ = lax.broadcasted_iota(jnp.int32, (c, w), 1) % c
    state_diag = ((lax.broadcasted_iota(jnp.int32, (w, 2 * GLA_DV), 0) < GLA_DK)
                  == (lax.broadcasted_iota(jnp.int32, (w, 2 * GLA_DV), 1) < GLA_DV))

    def split_store(dst, b, x0, x1):
        for cc in range(cpb):
            base = (b * cpb + cc) * 2 * c
            dst[pl.ds(pl.multiple_of(base, c), c), :] = x0[cc * c:(cc + 1) * c]
            dst[pl.ds(pl.multiple_of(base + c, c), c), :] = x1[cc * c:(cc + 1) * c]

    def vprep(b, carry):
        v = v_ref[pl.ds(pl.multiple_of(b * blk, blk), blk), :]
        zero = jnp.zeros_like(v)
        split_store(vbd_s, b, jnp.where(head0_v, v, zero), jnp.where(head0_v, zero, v))
        return carry

    lax.fori_loop(0, r // blk, vprep, 0)
    og_s[...] = jnp.zeros_like(og_s)

    dirs = (0, 1)
    tri_m = [jnp.where(same_chunk, jnp.where((bj_ <= bi_) if d == 0 else (bj_ >= bi_), 1.0, 0.0),
                       0.0).astype(BF16) for d in dirs]
    causal = [(cj_ <= ci_) if d == 0 else (cj_ >= ci_) for d in dirs]

    def prep(blocks):
        block_of = dict(enumerate(blocks))
        blocks = list(block_of)
        jobs = [(b, d) for b in blocks for d in dirs]
        rows = {b: pl.ds(pl.multiple_of(block_of[b] * blk, blk), blk) for b in blocks}
        ga = {b: ga_ref[rows[b], :] for b in blocks}
        logit = {j: _dot(ga[j[0]], wa_ref[j[1]]) + ba_ref[j[1]] for j in jobs}
        la = {j: (jnp.minimum(x, 0.0) - jnp.log(1.0 + jnp.exp(-jnp.abs(x)))) * (1.0 / GLA_TAU)
              for j, x in logit.items()}
        hl = {}
        for j in jobs:
            hi = la[j].astype(BF16)
            hl[j] = jnp.concatenate([hi, (la[j] - hi.astype(F32)).astype(BF16)], axis=1)
        cs = {j: _dot(tri_m[j[1]], hl[j]) for j in jobs}
        ts = {j: _dot(ones_m, hl[j]) for j in jobs}
        totb = {j: _dot_tn(hl[j], ind) for j in jobs}
        q = {b: q_ref[rows[b], :] * (GLA_DK ** -0.5) for b in blocks}
        k = {b: k_ref[rows[b], :] for b in blocks}
        for j in jobs:
            b, d = j
            first_chunk = block_of[b] * cpb
            bcum = cs[j][:, :w] + cs[j][:, w:]
            tot = ts[j][:, :w] + ts[j][:, w:]
            for cc in range(cpb):
                dec_s[d, first_chunk + cc] = jnp.exp(totb[j][:w, cc * w:(cc + 1) * w]
                                                 + totb[j][w:, cc * w:(cc + 1) * w])
            qt_s[d, rows[b], :] = (q[b] * jnp.exp(bcum)).astype(BF16)
            ke = (k[b] * jnp.exp(tot - bcum)).astype(BF16)
            kt = (k[b] * jnp.exp(-bcum)).astype(BF16)
            zero = jnp.zeros_like(kt)
            kt0, kt1 = jnp.where(head0_k, kt, zero), jnp.where(head0_k, zero, kt)
            for cc in range(cpb):
                crow = slice(cc * c, (cc + 1) * c)
                dst_rows = pl.ds(pl.multiple_of((first_chunk + cc) * 2 * c, 2 * c), 2 * c)
                kts_s[d, dst_rows, :] = jnp.concatenate([kt0[crow], kt1[crow]], axis=0).T
                ket_s[d, dst_rows, :] = ke[crow].T

    n_blocks = r // blk

    def prep_pair(t, carry):
        prep([2 * t, 2 * t + 1])
        return carry

    lax.fori_loop(0, n_blocks // 2, prep_pair, 0)
    if n_blocks % 2:
        prep([n_blocks - 1])

    def scan(t, sts):
        work = []
        for u in range(GLA_UNROLL):
            i = t * GLA_UNROLL + u
            for d in dirs:
                if d == 0:
                    n = i
                else:
                    n = jnp.where(i < n_ctx, n_ctx - 1 - i, n_chunks - 1 - (i - n_ctx))
                rows = pl.ds(pl.multiple_of(n * c, c), c)
                rows2 = pl.ds(pl.multiple_of(n * 2 * c, 2 * c), 2 * c)
                qc = qt_s[d, rows, :]
                a = jnp.where(causal[d], _dot(qc, kts_s[d, rows2, :]), 0.0).astype(BF16)
                dst = jnp.where(state_diag, _dot(ket_s[d, rows2, :], v_ref[rows, :]), 0.0)
                work.append((d, n, rows, rows2, qc, a, dst))
        sts = list(sts)
        for d, n, rows, rows2, qc, a, dst in work:
            o = _dot(jnp.concatenate([a, qc], axis=1),
                     jnp.concatenate([vbd_s[rows2, :], sts[d].astype(BF16)], axis=0))
            og_s[rows, :] = og_s[rows, :] + o
            dec = dec_s[d, n]
            sts[d] = sts[d] * jnp.concatenate([dec, dec], axis=1) + dst
        return tuple(sts)

    zero_state = jnp.zeros((w, 2 * GLA_DV), F32)
    lax.fori_loop(0, n_chunks // GLA_UNROLL, scan, (zero_state, zero_state))

    for hh in range(2):
        vc = slice(hh * GLA_DV, (hh + 1) * GLA_DV)
        on = _rms(og_s[:, vc], go_ref[...])
        o_ref[:, vc] = (on * _silu(gr_ref[:, vc])).astype(o_ref.dtype)


def _gla(gq, gk, gv, ga, gr, wa, ba, g_o, *, n_batch, lc, l):
    t_all = gq.shape[0]
    r = lc + l
    blk = 256 if r % 256 == 0 else 128
    kern = functools.partial(_gla_kernel, lc=lc, l=l, blk=blk)
    seq = lambda w, **kw: pl.BlockSpec((r, w), lambda b, hp: (b, hp), **kw)
    return pl.pallas_call(
        kern,
        grid=(n_batch, GLA_HEADS // 2),
        in_specs=[seq(2 * GLA_DK), seq(2 * GLA_DK), seq(2 * GLA_DV),
                  pl.BlockSpec((r, 128), lambda b, hp: (b, 0)),
                  seq(2 * GLA_DV, pipeline_mode=pl.Buffered(1)),
                  pl.BlockSpec((2, 128, 2 * GLA_DK), lambda b, hp: (0, 0, hp)),
                  pl.BlockSpec((2, 1, 2 * GLA_DK), lambda b, hp: (0, 0, hp)),
                  pl.BlockSpec((1, GLA_DV), lambda b, hp: (0, 0))],
        out_specs=pl.BlockSpec((r, 2 * GLA_DV), lambda b, hp: (b, hp)),
        out_shape=jax.ShapeDtypeStruct((t_all, GLA_HEADS * GLA_DV), BF16),
        scratch_shapes=[pltpu.VMEM((2, r, 2 * GLA_DK), BF16), pltpu.VMEM((2, 2 * r, 2 * GLA_DK), BF16),
                        pltpu.VMEM((2, 2 * r, GLA_CHUNK), BF16),
                        pltpu.VMEM((2, r // GLA_CHUNK, 2 * GLA_DK, 2 * GLA_DK), F32),
                        pltpu.VMEM((2 * r, 2 * GLA_DV), BF16), pltpu.VMEM((r, 2 * GLA_DV), F32)],
        compiler_params=_cparams(("arbitrary", "arbitrary")),
        name="gla_bidir",
    )(gq, gk, gv, ga, gr, wa, ba, g_o)


def _merge_kernel(h_ref, modb_ref, modc_ref, g_ref, oa_ref, op_ref, og_ref, wg_ref, wb_ref,
                  wo_ref, o_ref, m_ref, *, tm, lc, tiles_per_batch):
    d = h_ref.shape[-1]
    j = pl.program_id(0) % tiles_per_batch
    shift, scale, gate = _mod_rows(modb_ref, modc_ref, j, tm, d, lc)
    u = (_rms(h_ref[...], g_ref[...]) * (1.0 + scale) + shift).astype(BF16)
    for c0 in range(0, d, MERGE_COLS):
        cols = slice(c0, c0 + MERGE_COLS)
        m = None
        for br, src in enumerate((oa_ref, op_ref, og_ref)):
            gt = _sigmoid(_dot(u, wg_ref[:, br * d + c0:br * d + c0 + MERGE_COLS]))
            term = gt * _dot(src[...], wb_ref[br, :, cols])
            m = term if m is None else m + term
        m_ref[:, cols] = m.astype(BF16)
    y = _dot(m_ref[...], wo_ref[...])
    o_ref[...] = h_ref[...] + gate * y


def _merge(h, mod, layer, g_mix, o_a, o_p, o_g, w_gates, w_branch, w_out, *, n_batch, lc, l):
    t_all, d = h.shape
    r = lc + l
    tm = r // 4
    tiles_per_batch = r // tm
    kern = functools.partial(_merge_kernel, tm=tm, lc=lc, tiles_per_batch=tiles_per_batch)
    tok = lambda w: pl.BlockSpec((tm, w), lambda i: (i, 0))
    return pl.pallas_call(
        kern,
        grid=(t_all // tm,),
        in_specs=[tok(d)] + _mod_specs(mod, layer, 1, tiles_per_batch, n_batch) + [
            _const_spec((1, d)), tok(BRANCH_W), tok(BRANCH_W), tok(BRANCH_W),
            _const_spec(w_gates.shape), _const_spec(w_branch.shape), _const_spec(w_out.shape)],
        out_specs=tok(d),
        out_shape=jax.ShapeDtypeStruct((t_all, d), F32),
        scratch_shapes=[pltpu.VMEM((tm, d), BF16)],
        compiler_params=_cparams(("arbitrary",)),
        name="branch_merge",
    )(h, mod, mod, g_mix, o_a, o_p, o_g, w_gates, w_branch, w_out)


_ROPE_PARTNER = np.arange(MLA_ROPE) ^ (AXIS_DIM // 2)


def _rope_table(lc, l):
    pos = jnp.arange(l)
    row = (pos // GRID_W).astype(F32)
    col = (pos % GRID_W).astype(F32)
    inv = ROPE_THETA ** (-jnp.arange(0, AXIS_DIM, 2, dtype=F32) / AXIS_DIM)
    ang = jnp.stack([row, col], axis=-1)[:, :, None] * inv
    cos = jnp.repeat(jnp.cos(ang)[:, :, None, :], 2, axis=2).reshape(l, MLA_ROPE)
    sin = jnp.repeat(jnp.sin(ang)[:, :, None, :], 2, axis=2)
    sin = (sin * jnp.array([-1.0, 1.0], F32)[None, None, :, None]).reshape(l, MLA_ROPE)
    pad = HEAD_LANES - MLA_QK
    tab_c = jnp.concatenate([
        jnp.concatenate([jnp.ones((lc, MLA_QK), F32), jnp.zeros((lc, pad), F32)], axis=1),
        jnp.concatenate([jnp.ones((l, MLA_NOPE), F32), cos, jnp.zeros((l, pad), F32)], axis=1)],
        axis=0)
    tab_s = jnp.concatenate([
        jnp.zeros((lc, HEAD_LANES), F32),
        jnp.concatenate([jnp.zeros((l, MLA_NOPE), F32), sin, jnp.zeros((l, pad), F32)], axis=1)],
        axis=0)
    return tab_c, tab_s


def _pack_layer(p, layer):
    d = p["w_in"].shape[1]
    w_in = p["w_in"][layer]
    offs = np.cumsum([0, MLA_Q_RANK, MLA_KV_RANK, MLA_ROPE, BRANCH_W, GLA_HEADS * GLA_DK,
                      GLA_HEADS * GLA_DK, GLA_HEADS * GLA_DV, 2 * GLA_GATE_RANK,
                      GLA_HEADS * GLA_DV, N_BRANCH * d])
    cq, ckv, kr, pz, gq, gk, gv, ga, gr, gates = [w_in[:, offs[i]:offs[i + 1]] for i in range(10)]
    pad = HEAD_LANES - MLA_QK

    def rope_lanes(x):
        lead = x.shape[:-1]
        return jnp.concatenate([jnp.zeros(lead + (MLA_NOPE,), F32), x, jnp.zeros(lead + (pad,), F32)],
                               axis=-1)

    ga_ext = jnp.concatenate([ga, jnp.zeros((d, 128 - 2 * GLA_GATE_RANK), F32)], axis=1)
    w_all = jnp.concatenate([cq, ckv, rope_lanes(kr), rope_lanes(kr[:, _ROPE_PARTNER]),
                             pz, gq, gk, gv, ga_ext, gr], axis=1).astype(BF16)

    w_uq = p["w_uq"][layer].reshape(MLA_Q_RANK, MLA_HEADS, MLA_QK)
    w_uq_a = jnp.concatenate([w_uq, jnp.zeros((MLA_Q_RANK, MLA_HEADS, pad), F32)], axis=2)
    w_uq_b = rope_lanes(w_uq[:, :, MLA_NOPE + _ROPE_PARTNER])
    w_uq = jnp.concatenate([w_uq_a.reshape(MLA_Q_RANK, -1), w_uq_b.reshape(MLA_Q_RANK, -1)],
                           axis=1).astype(BF16)
    w_ukv = p["w_ukv"][layer].reshape(MLA_KV_RANK, MLA_HEADS, MLA_NOPE + MLA_V)
    w_uk = jnp.concatenate([w_ukv[:, :, :MLA_NOPE],
                            jnp.zeros((MLA_KV_RANK, MLA_HEADS, HEAD_LANES - MLA_NOPE), F32)], axis=2)
    w_uk = w_uk.reshape(MLA_KV_RANK, MLA_HEADS * HEAD_LANES).astype(BF16)
    w_uv = w_ukv[:, :, MLA_NOPE:].reshape(MLA_KV_RANK, MLA_HEADS * MLA_V).astype(BF16)

    def gain_tab(g):
        return jnp.stack([jnp.concatenate([g, jnp.zeros((pad,), F32)]),
                          rope_lanes(g[MLA_NOPE + _ROPE_PARTNER])])

    wa = jnp.zeros((2, 128, GLA_HEADS * GLA_DK), F32)
    for dr in range(2):
        wa = wa.at[dr, dr * GLA_GATE_RANK:(dr + 1) * GLA_GATE_RANK].set(p["w_a2"][layer, dr])

    return dict(
        g_mix=p["g_mix"][layer][None, :], w_all=w_all,
        g_cq=p["g_cq"][layer][None, :], w_uq=w_uq,
        g_ckv=p["g_ckv"][layer][None, :], w_uk=w_uk, w_uv=w_uv,
        gq_tab=gain_tab(p["g_qn"][layer]) * (MLA_QK ** -0.5 * math.log2(math.e)), gk_tab=gain_tab(p["g_kn"][layer]),
        w_gates=gates.astype(BF16),
        w_pool=p["w_pool"][layer].astype(BF16), pool_scale=p["pool_scale"][layer][None, :],
        wa=wa.astype(BF16), ba=p["b_a2"][layer][:, None, :], g_gla_o=p["g_gla_o"][layer][None, :],
        w_branch=p["w_branch"][layer].astype(BF16), w_out=p["w_out"][layer].astype(BF16),
    )


def kernel(x, c, ctx, c_ctx, w_ada, b_ada, g_ffn1, ffn1_w1, ffn1_w3, ffn1_w2, g_mix, w_in,
           g_cq, w_uq, g_ckv, w_ukv, g_qn, g_kn, w_pool, pool_scale, w_a2, b_a2, g_gla_o,
           w_branch, w_out, g_ffn2, ffn2_w1, ffn2_w3, ffn2_w2):
    n_batch, l, d = x.shape
    lc = ctx.shape[1]
    depth = w_ada.shape[0]
    dims = dict(n_batch=n_batch, lc=lc, l=l)
    p = dict(w_in=w_in, g_mix=g_mix, g_cq=g_cq, w_uq=w_uq, g_ckv=g_ckv, w_ukv=w_ukv, g_qn=g_qn,
             g_kn=g_kn, w_pool=w_pool, pool_scale=pool_scale, w_a2=w_a2, b_a2=b_a2,
             g_gla_o=g_gla_o, w_branch=w_branch, w_out=w_out)

    cond_rows = 16
    cond = jnp.zeros((cond_rows, d), F32).at[:n_batch].set(c).at[n_batch].set(c_ctx)
    mod = _ada_mod(cond, w_ada, b_ada).reshape(depth, cond_rows, 3, 3, d)

    tab = _rope_table(lc, l)
    f1_w1, f1_w3, f2_w1, f2_w3 = _cast_bf16(ffn1_w1, ffn1_w3, ffn2_w1, ffn2_w3)
    f1_w2, f2_w2 = _cast_bf16(ffn1_w2, ffn2_w2)
    h = x.reshape(n_batch * l, d)
    for layer in range(depth):
        last = layer == depth - 1
        pw = _pack_layer(p, layer)
        first = dict(ctx_src=ctx.reshape(n_batch * lc, d)) if layer == 0 else {}
        h = _ffn(h, mod, layer, 0, g_ffn1[layer], f1_w1, f1_w3, f1_w2, **first, **dims)
        q, k, v, pz, gq, gk, gv, ga, gr = _mixin(h, mod, layer, pw, tab, **dims)
        o_a = _attention(q, k, v, **dims)
        o_p = _pool(pz, pw["w_pool"], pw["pool_scale"], **dims)
        o_g = _gla(gq, gk, gv, ga, gr, pw["wa"], pw["ba"], pw["g_gla_o"], **dims)
        h = _merge(h, mod, layer, pw["g_mix"], o_a, o_p, o_g, pw["w_gates"], pw["w_branch"],
                   pw["w_out"], **dims)
        h = _ffn(h, mod, layer, 2, g_ffn2[layer], f2_w1, f2_w3, f2_w2, lat_only=last, **dims)
    return h.reshape(n_batch, l, d)
```
